```python
import jax, jax.numpy as jnp
from jax import lax
import numpy as np

D_MODEL = 4096
BATCH = 2
SEQ = 4096
DEPTH = 2
DEC_BATCH = 32
DEC_SEQ = 64
PAST_LEN = 4096

CHUNK = 64
N_EVEN = (DEPTH + 1) // 2
N_ODD = DEPTH // 2
EPS = 1e-6

A_WIDTH = D_MODEL // 2
HEAD_DIM = 128
N_HEADS_A = A_WIDTH // HEAD_DIM
Q_BLOCK = 128
ATTN_SCALE = HEAD_DIM ** -0.5
B_WIDTH = D_MODEL // 2
POOL_WINDOWS = (2, 4, 8, 16)
N_POOL_GROUPS = len(POOL_WINDOWS)
POOL_GROUP = B_WIDTH // N_POOL_GROUPS
POOL_HIST = max(POOL_WINDOWS) - 1
GATE0_WIDTH = A_WIDTH + B_WIDTH
SPLIT0 = [A_WIDTH, 2 * A_WIDTH, 3 * A_WIDTH, 3 * A_WIDTH + N_HEADS_A, 3 * A_WIDTH + N_HEADS_A + B_WIDTH]
IN0_WIDTH = 3 * A_WIDTH + N_HEADS_A + B_WIDTH + GATE0_WIDTH
C_WIDTH = D_MODEL
SGU_CHUNK = 128
N_SGU_GROUPS = 16
SGU_GROUP = C_WIDTH // N_SGU_GROUPS
IN1_WIDTH = 3 * C_WIDTH

kernel_name = 'fox_pool_sgu_stream_step'


def rms_norm(x, g):
    xf = x.astype(jnp.float32)
    y = xf * lax.rsqrt(jnp.mean(xf * xf, axis=-1, keepdims=True) + EPS)
    return (y * g.astype(jnp.float32)).astype(x.dtype)


def layer_norm(x, g, b):
    xf = x.astype(jnp.float32)
    xc = xf - jnp.mean(xf, axis=-1, keepdims=True)
    var = jnp.mean(xc * xc, axis=-1, keepdims=True)
    return (xc * lax.rsqrt(var + EPS) * g.astype(jnp.float32) + b.astype(jnp.float32)).astype(x.dtype)


def modulate(x, c, w_ada, b_ada, g_norm):
    mod = jax.nn.silu(c) @ w_ada + b_ada
    shift, scale, gate = jnp.split(mod[:, None, :], 3, axis=-1)
    h = rms_norm(x, g_norm) * (1 + scale) + shift
    return h, gate


def fox_prompt(q, k, v, logf):
    bsz, seq, nh, hd = q.shape
    nb = seq // Q_BLOCK
    ft = jnp.cumsum(logf, axis=1).transpose(0, 2, 1)
    kpos = jnp.arange(seq)
    qb = q.reshape(bsz, nb, Q_BLOCK, nh, hd).transpose(1, 0, 2, 3, 4)
    fb = ft.reshape(bsz, nh, nb, Q_BLOCK).transpose(2, 0, 1, 3)
    pb = kpos.reshape(nb, Q_BLOCK)

    def block(args):
        qi, fi, pi = args
        s = jnp.einsum('bqhd,bkhd->bhqk', qi, k, preferred_element_type=jnp.float32) * ATTN_SCALE
        s = s + fi[..., :, None] - ft[:, :, None, :]
        s = jnp.where(kpos[None, None, None, :] <= pi[None, None, :, None], s, -jnp.inf)
        p = jax.nn.softmax(s, axis=-1)
        return jnp.einsum('bhqk,bkhd->bqhd', p.astype(v.dtype), v)

    o = lax.map(block, (qb, fb, pb))
    return o.transpose(1, 0, 2, 3, 4).reshape(bsz, seq, nh * hd)


def fox_sample(q, k, v, logf, ck, cv, clogf):
    bsz, L, nh, hd = q.shape
    fn = jnp.cumsum(logf, axis=1).transpose(0, 2, 1)
    clf = clogf.astype(jnp.float32)
    g_past = (lax.cumsum(clf, axis=1, reverse=True) - clf).transpose(0, 2, 1)
    s_past = jnp.einsum('bqhd,bkhd->bhqk', q, ck, preferred_element_type=jnp.float32) * ATTN_SCALE
    s_past = s_past + fn[..., :, None] + g_past[:, :, None, :]
    s_new = jnp.einsum('bqhd,bkhd->bhqk', q, k, preferred_element_type=jnp.float32) * ATTN_SCALE
    s_new = s_new + fn[..., :, None] - fn[:, :, None, :]
    causal = jnp.tril(jnp.ones((L, L), dtype=bool))
    s_new = jnp.where(causal[None, None], s_new, -jnp.inf)
    m = jnp.maximum(s_past.max(-1, keepdims=True), s_new.max(-1, keepdims=True))
    p_past = jnp.exp(s_past - m)
    p_new = jnp.exp(s_new - m)
    denom = p_past.sum(-1, keepdims=True) + p_new.sum(-1, keepdims=True)
    o = (jnp.einsum('bhqk,bkhd->bhqd', p_past.astype(cv.dtype), cv, preferred_element_type=jnp.float32)
         + jnp.einsum('bhqk,bkhd->bhqd', p_new.astype(v.dtype), v, preferred_element_type=jnp.float32)) / denom
    return o.transpose(0, 2, 1, 3).reshape(bsz, L, nh * hd).astype(v.dtype)


def pool_mix(u, hist, pos0, w_pool, ls_pool):
    bsz, L, _ = u.shape
    ext = jnp.concatenate([hist.astype(u.dtype), u], axis=1)
    cs = jnp.cumsum(ext.astype(jnp.float32), axis=1)
    cs = jnp.concatenate([jnp.zeros_like(cs[:, :1]), cs], axis=1)
    pos = pos0 + jnp.arange(L)
    means = []
    for gi, w in enumerate(POOL_WINDOWS):
        sl = slice(gi * POOL_GROUP, (gi + 1) * POOL_GROUP)
        win = cs[:, POOL_HIST + 1:POOL_HIST + 1 + L, sl] - cs[:, POOL_HIST + 1 - w:POOL_HIST + 1 - w + L, sl]
        cnt = jnp.minimum(w, pos + 1).astype(jnp.float32)
        means.append(win / cnt[None, :, None])
    pooled = jnp.stack(means, axis=2)
    d = pooled - u.reshape(bsz, L, N_POOL_GROUPS, POOL_GROUP).astype(jnp.float32)
    y = jnp.einsum('blgc,gce->blge', d.astype(u.dtype), w_pool).reshape(bsz, L, B_WIDTH) * ls_pool
    return y, ext[:, -POOL_HIST:]


def layer_ab(h, w_in, b_f, g_q, g_k, w_pool, ls_pool, w_out, pool_hist, pos0, kv_cache):
    bsz, L, _ = h.shape
    z = h @ w_in
    q, k, v, f, u_b, gate = jnp.split(z, SPLIT0, axis=-1)
    q = rms_norm(q.reshape(bsz, L, N_HEADS_A, HEAD_DIM), g_q)
    k = rms_norm(k.reshape(bsz, L, N_HEADS_A, HEAD_DIM), g_k)
    v = v.reshape(bsz, L, N_HEADS_A, HEAD_DIM)
    logf = jax.nn.log_sigmoid(f.astype(jnp.float32) + b_f.astype(jnp.float32))
    if kv_cache is None:
        a = fox_prompt(q, k, v, logf)
    else:
        a = fox_sample(q, k, v, logf, kv_cache[0], kv_cache[1], kv_cache[2])
    b, new_hist = pool_mix(u_b, pool_hist, pos0, w_pool, ls_pool)
    mixed = jnp.concatenate([a, b.astype(a.dtype)], axis=-1) * jax.nn.silu(gate)
    return mixed @ w_out, k, v, logf, new_hist


def layer_c(h, w_in, g_v, b_v, w_s, b_s, w_out):
    bsz, L, _ = h.shape
    z = h @ w_in
    uv = jax.nn.gelu(z[..., :2 * C_WIDTH], approximate=False)
    gate = z[..., 2 * C_WIDTH:]
    u, v = uv[..., :C_WIDTH], uv[..., C_WIDTH:]
    v = layer_norm(v, g_v, b_v)
    cl = min(L, SGU_CHUNK)
    vc = v.reshape(bsz, L // cl, cl, N_SGU_GROUPS, SGU_GROUP)
    mask = jnp.tril(jnp.ones((cl, cl), dtype=bool))
    ws = jnp.where(mask[None], w_s[:, :cl, :cl], 0)
    sv = jnp.einsum('gts,bnsgd->bntgd', ws, vc) + b_s[:, :cl].T[None, None, :, :, None]
    y = (u * sv.reshape(bsz, L, C_WIDTH) * jax.nn.silu(gate)) @ w_out
    return y, v


def setup_inputs(seed: int = 0) -> dict:
    key = jax.random.key(seed)
    ks = jax.random.split(key, 26)

    def nrm(k, shape, s):
        return jax.random.normal(k, shape, jnp.float32) * s

    return {
        'x_prompt': nrm(ks[0], (BATCH, SEQ, D_MODEL), 1.0),
        'x_sample': nrm(ks[1], (DEC_BATCH, DEC_SEQ, D_MODEL), 1.0),
        'cache_k': nrm(ks[2], (N_EVEN, DEC_BATCH, PAST_LEN, N_HEADS_A, HEAD_DIM), 1.0),
        'cache_v': nrm(ks[3], (N_EVEN, DEC_BATCH, PAST_LEN, N_HEADS_A, HEAD_DIM), 1.0),
        'cache_logf': jax.nn.log_sigmoid(2.5 + nrm(ks[4], (N_EVEN, DEC_BATCH, PAST_LEN, N_HEADS_A), 1.0)),
        'state_pool': nrm(ks[5], (N_EVEN, DEC_BATCH, POOL_HIST, B_WIDTH), 1.0),
        'c_prompt': nrm(ks[6], (BATCH, D_MODEL), 1.0),
        'c_sample': nrm(ks[7], (DEC_BATCH, D_MODEL), 1.0),
        'w_ada': nrm(ks[8], (DEPTH, D_MODEL, 3 * D_MODEL), 0.5 * D_MODEL ** -0.5),
        'b_ada': nrm(ks[9], (DEPTH, 3 * D_MODEL), 0.02),
        'g_norm': 1.0 + nrm(ks[10], (DEPTH, D_MODEL), 0.1),
        'w_in_ab': nrm(ks[11], (N_EVEN, D_MODEL, IN0_WIDTH), D_MODEL ** -0.5),
        'b_forget': jax.random.uniform(ks[12], (N_EVEN, N_HEADS_A), jnp.float32, 1.0, 4.0),
        'g_q': 1.0 + nrm(ks[13], (N_EVEN, HEAD_DIM), 0.1),
        'g_k': 1.0 + nrm(ks[14], (N_EVEN, HEAD_DIM), 0.1),
        'w_pool': nrm(ks[15], (N_EVEN, N_POOL_GROUPS, POOL_GROUP, POOL_GROUP), POOL_GROUP ** -0.5),
        'ls_pool': 1.0 + nrm(ks[16], (N_EVEN, B_WIDTH), 0.1),
        'w_out_ab': nrm(ks[17], (N_EVEN, A_WIDTH + B_WIDTH, D_MODEL), (A_WIDTH + B_WIDTH) ** -0.5),
        'w_in_c': nrm(ks[18], (N_ODD, D_MODEL, IN1_WIDTH), D_MODEL ** -0.5),
        'g_v': 1.0 + nrm(ks[19], (N_ODD, C_WIDTH), 0.1),
        'b_v': nrm(ks[20], (N_ODD, C_WIDTH), 0.02),
        'w_s': nrm(ks[21], (N_ODD, N_SGU_GROUPS, SGU_CHUNK, SGU_CHUNK), SGU_CHUNK ** -0.5),
        'b_s': 1.0 + nrm(ks[22], (N_ODD, N_SGU_GROUPS, SGU_CHUNK), 0.1),
        'w_out_c': nrm(ks[23], (N_ODD, C_WIDTH, D_MODEL), C_WIDTH ** -0.5),
    }


def reference(x_prompt, x_sample, cache_k, cache_v, cache_logf, state_pool, c_prompt, c_sample,
              w_ada, b_ada, g_norm, w_in_ab, b_forget, g_q, g_k, w_pool, ls_pool, w_out_ab,
              w_in_c, g_v, b_v, w_s, b_s, w_out_c):
    yp, ys = x_prompt, x_sample
    kp_l, vp_l, lfp_l, pp_l = [], [], [], []
    ks_l, vs_l, lfs_l, ps_l, cs_l = [], [], [], [], []
    for layer in range(DEPTH):
        hp, gp = modulate(yp, c_prompt, w_ada[layer], b_ada[layer], g_norm[layer])
        hs, gs = modulate(ys, c_sample, w_ada[layer], b_ada[layer], g_norm[layer])
        i = layer // 2
        if layer % 2 == 0:
            zero_hist = jnp.zeros((yp.shape[0], POOL_HIST, B_WIDTH), yp.dtype)
            op, kp, vp, lfp, pp = layer_ab(hp, w_in_ab[i], b_forget[i], g_q[i], g_k[i], w_pool[i], ls_pool[i],
                                          w_out_ab[i], zero_hist, 0, None)
            os_, ks_, vs_, lfs, ps = layer_ab(hs, w_in_ab[i], b_forget[i], g_q[i], g_k[i], w_pool[i], ls_pool[i],
                                             w_out_ab[i], state_pool[i], PAST_LEN,
                                             (cache_k[i], cache_v[i], cache_logf[i]))
            kp_l.append(kp); vp_l.append(vp); lfp_l.append(lfp); pp_l.append(pp)
            ks_l.append(ks_); vs_l.append(vs_); lfs_l.append(lfs); ps_l.append(ps)
        else:
            op, _ = layer_c(hp, w_in_c[i], g_v[i], b_v[i], w_s[i], b_s[i], w_out_c[i])
            os_, v_c = layer_c(hs, w_in_c[i], g_v[i], b_v[i], w_s[i], b_s[i], w_out_c[i])
            cs_l.append(v_c)
        yp = yp + gp * op
        ys = ys + gs * os_
    k_prompt = jnp.stack(kp_l)
    v_prompt = jnp.stack(vp_l)
    logf_prompt = jnp.stack(lfp_l)
    pool_prompt = jnp.stack(pp_l)
    k_sample = jnp.stack(ks_l)
    v_sample = jnp.stack(vs_l)
    logf_sample = jnp.stack(lfs_l)
    pool_sample = jnp.stack(ps_l)
    sgu_v_sample = jnp.stack(cs_l)
    return (yp, ys, k_prompt, v_prompt, logf_prompt, pool_prompt,
            k_sample, v_sample, logf_sample, pool_sample, sgu_v_sample)
```

```python
import functools

import jax
import jax.numpy as jnp
from jax import lax
from jax.experimental import pallas as pl
from jax.experimental.pallas import tpu as pltpu

F32 = jnp.float32
BF16 = jnp.bfloat16

EPS = 1e-6
HEAD_DIM = 128
POOL_WINDOWS = (2, 4, 8, 16)
POOL_HIST = max(POOL_WINDOWS) - 1
HALO = POOL_HIST + 1
SGU_CHUNK = 128
N_SGU_GROUPS = 16
NEG = -1e30
LANE = 128
MIB = 1024 * 1024


def _params(sem, vmem_mib):
    return pltpu.CompilerParams(dimension_semantics=sem, vmem_limit_bytes=vmem_mib * MIB)


def _dot(a, b):
    return jnp.dot(a, b, preferred_element_type=F32)


def _dot_nt(a, b):
    return lax.dot_general(a, b, (((1,), (1,)), ((), ())), preferred_element_type=F32)


def _ada_body(c_ref, w_ref, b_ref, o_ref):
    a = jax.nn.silu(c_ref[...]).astype(BF16)
    o_ref[0] = _dot(a, w_ref[0].astype(BF16)) + b_ref[0]


def _ada_call(c_all, w_ada, b_ada, tn=512):
    depth, d, n = w_ada.shape
    rp = c_all.shape[0]
    return pl.pallas_call(
        _ada_body,
        grid=(depth, n // tn),
        in_specs=[pl.BlockSpec((rp, d), lambda l, j: (0, 0)),
                  pl.BlockSpec((1, d, tn), lambda l, j: (l, 0, j)),
                  pl.BlockSpec((1, 1, tn), lambda l, j: (l, 0, j))],
        out_specs=pl.BlockSpec((1, rp, tn), lambda l, j: (l, 0, j)),
        out_shape=jax.ShapeDtypeStruct((depth, rp, n), F32),
        compiler_params=_params(("arbitrary", "arbitrary"), 40),
        name="ada_mod",
    )(c_all, w_ada, b_ada.reshape(depth, 1, n))


def _normmod_body(x_ref, g_ref, sc_ref, sh_ref, o_ref):
    x = x_ref[...]
    y = x * lax.rsqrt(jnp.mean(x * x, axis=-1, keepdims=True) + EPS) * g_ref[...]
    o_ref[...] = (y * (1 + sc_ref[...]) + sh_ref[...]).astype(BF16)


def _normmod_call(x, g, scale, shift, bb, lb):
    b, l, d = x.shape
    row = pl.BlockSpec((bb, lb, d), lambda i, r: (i, r, 0))
    per_b = pl.BlockSpec((bb, 1, d), lambda i, r: (i, 0, 0))
    return pl.pallas_call(
        _normmod_body,
        grid=(b // bb, l // lb),
        in_specs=[row, pl.BlockSpec((1, 1, d), lambda i, r: (0, 0, 0)), per_b, per_b],
        out_specs=row,
        out_shape=jax.ShapeDtypeStruct((b, l, d), BF16),
        compiler_params=_params(("arbitrary", "arbitrary"), 40),
        name="norm_mod",
    )(x, g.reshape(1, 1, d), scale, shift)


def _proj_body(a_ref, w_ref, *rest, mode, n_gelu_tiles):
    acc = _dot(a_ref[...], w_ref[...])
    tn = acc.shape[1]
    if mode == "headnorm":
        g_ref, outs = rest[0], rest[1:]
        for hh in range(tn // HEAD_DIM):
            cs = slice(hh * HEAD_DIM, (hh + 1) * HEAD_DIM)
            blk = acc[:, cs]
            y = blk * lax.rsqrt(jnp.mean(blk * blk, axis=-1, keepdims=True) + EPS) * g_ref[...]
            for o_ref in outs:
                o_ref[:, cs] = y.astype(o_ref.dtype)
    elif mode == "plain":
        for o_ref in rest:
            o_ref[...] = acc.astype(o_ref.dtype)
    elif mode == "logsigmoid":
        b_ref, o_ref = rest
        x = acc + b_ref[...]
        o_ref[...] = jnp.minimum(x, 0.0) - jnp.log1p(jnp.exp(-jnp.abs(x)))
    elif mode == "silu":
        (o_ref,) = rest
        o_ref[...] = jax.nn.silu(acc)
    elif mode == "gelu_silu":
        (o_ref,) = rest
        j = pl.program_id(1)

        @pl.when(j < n_gelu_tiles)
        def _():
            o_ref[...] = 0.5 * acc * (1.0 + lax.erf(acc * (2.0 ** -0.5)))

        @pl.when(j >= n_gelu_tiles)
        def _():
            o_ref[...] = jax.nn.silu(acc)
    else:
        raise ValueError(mode)


def _proj_call(a, w, mode, out_dtypes, extra=None, n_gelu_tiles=0, tm=1024, tn=512, name="proj"):
    m, k = a.shape
    n = w.shape[1]
    tn = min(tn, n)
    in_specs = [pl.BlockSpec((tm, k), lambda i, j: (i, 0)),
                pl.BlockSpec((k, tn), lambda i, j: (0, j))]
    args = [a, w]
    if extra is not None:
        ew = extra.shape[1]
        if ew == n:
            in_specs.append(pl.BlockSpec((1, tn), lambda i, j: (0, j)))
        else:
            in_specs.append(pl.BlockSpec((1, ew), lambda i, j: (0, 0)))
        args.append(extra)
    out_spec = pl.BlockSpec((tm, tn), lambda i, j: (i, j))
    outs = pl.pallas_call(
        functools.partial(_proj_body, mode=mode, n_gelu_tiles=n_gelu_tiles),
        grid=(m // tm, n // tn),
        in_specs=in_specs,
        out_specs=[out_spec] * len(out_dtypes),
        out_shape=[jax.ShapeDtypeStruct((m, n), dt) for dt in out_dtypes],
        compiler_params=_params(("arbitrary", "arbitrary"), 48),
        name=name,
    )(*args)
    return outs


def _outproj_body(*refs, n_a):
    a_refs, w_refs = refs[:n_a], refs[n_a:2 * n_a]
    x_ref, g_ref, o_ref = refs[2 * n_a:]
    bb, lb, tn = x_ref.shape
    acc = None
    for a_ref, w_ref in zip(a_refs, w_refs):
        a = a_ref[...].reshape(bb * lb, a_ref.shape[2])
        d = _dot(a, w_ref[...])
        acc = d if acc is None else acc + d
    o_ref[...] = x_ref[...] + g_ref[...] * acc.reshape(bb, lb, tn)


def _outproj_call(a_list, w_list, x, gate, bb, lb, tn=512):
    b, l, n = x.shape
    n_a = len(a_list)
    in_specs = []
    for a in a_list:
        in_specs.append(pl.BlockSpec((bb, lb, a.shape[2]), lambda i, r, j: (i, r, 0)))
    for w in w_list:
        in_specs.append(pl.BlockSpec((w.shape[0], tn), lambda i, r, j: (0, j)))
    xspec = pl.BlockSpec((bb, lb, tn), lambda i, r, j: (i, r, j))
    in_specs += [xspec, pl.BlockSpec((bb, 1, tn), lambda i, r, j: (i, 0, j))]
    return pl.pallas_call(
        functools.partial(_outproj_body, n_a=n_a),
        grid=(b // bb, l // lb, n // tn),
        in_specs=in_specs,
        out_specs=xspec,
        out_shape=jax.ShapeDtypeStruct((b, l, n), F32),
        compiler_params=_params(("arbitrary", "arbitrary", "arbitrary"), 48),
        name="out_proj",
    )(*a_list, *w_list, x, gate)


def _cumsum_body(x_ref, o_ref, *, ch, reverse_exclusive):
    rb, n = x_ref.shape
    ii = lax.broadcasted_iota(jnp.int32, (ch, ch), 0)
    jj = lax.broadcasted_iota(jnp.int32, (ch, ch), 1)
    tri = (ii > jj) if reverse_exclusive else (ii <= jj)
    tri = tri.astype(F32)
    chunks = range(n // ch)
    carry = jnp.zeros((rb, 1), F32)
    for c in (reversed(chunks) if reverse_exclusive else chunks):
        xc = x_ref[:, c * ch:(c + 1) * ch]
        y = jnp.dot(xc, tri, precision=lax.Precision.HIGHEST, preferred_element_type=F32)
        o_ref[:, c * ch:(c + 1) * ch] = y + carry
        carry = carry + jnp.sum(xc, axis=-1, keepdims=True)


def _cumsum_call(x, reverse_exclusive, rb):
    r, n = x.shape
    ch = min(n, 512)
    return pl.pallas_call(
        functools.partial(_cumsum_body, ch=ch, reverse_exclusive=reverse_exclusive),
        grid=(r // rb,),
        in_specs=[pl.BlockSpec((rb, n), lambda i: (i, 0))],
        out_specs=pl.BlockSpec((rb, n), lambda i: (i, 0)),
        out_shape=jax.ShapeDtypeStruct((r, n), F32),
        compiler_params=_params(("arbitrary",), 32),
        name="cumsum",
    )(x)


def _softmax_update(s, v, m_ref, l_ref, acc_ref):
    m_prev = m_ref[...]
    m_new = jnp.maximum(m_prev, jnp.max(s, axis=-1, keepdims=True))
    alpha = jnp.exp(m_prev - m_new)
    p = jnp.exp(s - m_new)
    l_ref[...] = alpha * l_ref[...] + jnp.sum(p, axis=-1, keepdims=True)
    acc_ref[...] = alpha * acc_ref[...] + _dot(p.astype(BF16), v)
    m_ref[...] = m_new


def _attn_prompt_body(q_ref, k_ref, v_ref, fcol_ref, frow_ref, sg_ref, o_ref,
                      m_ref, l_ref, acc_ref, *, tq, scale):
    h = pl.program_id(1)
    qi = pl.program_id(2)
    q = q_ref[0]
    fblk = fcol_ref[0]
    lane = lax.broadcasted_iota(jnp.int32, fblk.shape, 1)
    fcol = jnp.sum(jnp.where(lane == h, fblk, 0.0), axis=-1, keepdims=True)
    m_ref[...] = jnp.full(m_ref.shape, NEG, F32)
    l_ref[...] = jnp.zeros(l_ref.shape, F32)
    acc_ref[...] = jnp.zeros(acc_ref.shape, F32)

    def step(kj, masked):
        off = pl.multiple_of(kj * tq, tq)
        k = k_ref[0, pl.ds(off, tq), :]
        v = v_ref[0, pl.ds(off, tq), :]
        s = _dot_nt(q, k) * scale + (fcol - frow_ref[0, kj])
        if masked:
            rr = lax.broadcasted_iota(jnp.int32, s.shape, 0)
            cc = lax.broadcasted_iota(jnp.int32, s.shape, 1)
            s = jnp.where(cc <= rr, s, NEG)
        _softmax_update(s, v, m_ref, l_ref, acc_ref)

    def loop_body(kj, carry):
        step(kj, False)
        return carry

    lax.fori_loop(0, qi, loop_body, 0)
    step(qi, True)
    o_ref[0] = (acc_ref[...] / l_ref[...] * sg_ref[0]).astype(BF16)


def _attn_prompt_call(q, k, v, fcol, frow, sgate, tq=256):
    b, s, aw = q.shape
    nh = aw // HEAD_DIM
    nq = s // tq
    qspec = pl.BlockSpec((1, tq, HEAD_DIM), lambda bi, h, qi: (bi, qi, h))
    kvspec = pl.BlockSpec((1, s, HEAD_DIM), lambda bi, h, qi: (bi, 0, h))
    frow4 = frow.reshape(b * nh, nq, 1, tq)
    return pl.pallas_call(
        functools.partial(_attn_prompt_body, tq=tq, scale=HEAD_DIM ** -0.5),
        grid=(b, nh, nq),
        in_specs=[qspec, kvspec, kvspec,
                  pl.BlockSpec((1, tq, nh), lambda bi, h, qi: (bi, qi, 0)),
                  pl.BlockSpec((1, nq, 1, tq), lambda bi, h, qi: (bi * nh + h, 0, 0, 0)),
                  qspec],
        out_specs=qspec,
        out_shape=jax.ShapeDtypeStruct((b, s, aw), BF16),
        scratch_shapes=[pltpu.VMEM((tq, 1), F32), pltpu.VMEM((tq, 1), F32),
                        pltpu.VMEM((tq, HEAD_DIM), F32)],
        compiler_params=_params(("arbitrary", "arbitrary", "arbitrary"), 32),
        name="attn_prompt",
    )(q, k, v, fcol, frow4, sgate)


def _attn_sample_body(q_ref, kn_ref, vn_ref, ck_ref, cv_ref, fcol_ref, frow_ref, gp_ref, sg_ref,
                      o_ref, m_ref, l_ref, acc_ref, *, nh, scale):
    pi = pl.program_id(1)

    @pl.when(pi == 0)
    def _():
        m_ref[...] = jnp.full(m_ref.shape, NEG, F32)
        l_ref[...] = jnp.zeros(l_ref.shape, F32)
        acc_ref[...] = jnp.zeros(acc_ref.shape, F32)

    def update(h, s, v):
        cs = slice(h * HEAD_DIM, (h + 1) * HEAD_DIM)
        m_prev = m_ref[:, cs]
        m_new = jnp.maximum(m_prev, jnp.max(s, axis=-1, keepdims=True))
        alpha = jnp.exp(m_prev - m_new)
        p = jnp.exp(s - m_new[:, :1])
        l_ref[:, cs] = alpha * l_ref[:, cs] + jnp.sum(p, axis=-1, keepdims=True)
        acc_ref[:, cs] = alpha * acc_ref[:, cs] + _dot(p.astype(BF16), v)
        m_ref[:, cs] = m_new

    for h in range(nh):
        cs = slice(h * HEAD_DIM, (h + 1) * HEAD_DIM)
        k = ck_ref[0, :, cs].astype(BF16)
        v = cv_ref[0, :, cs].astype(BF16)
        s = _dot_nt(q_ref[0, :, cs], k) * scale + (fcol_ref[0, :, h:h + 1] + gp_ref[0, h:h + 1, :])
        update(h, s, v)

    @pl.when(pi == pl.num_programs(1) - 1)
    def _():
        for h in range(nh):
            cs = slice(h * HEAD_DIM, (h + 1) * HEAD_DIM)
            s = (_dot_nt(q_ref[0, :, cs], kn_ref[0, :, cs]) * scale
                 + (fcol_ref[0, :, h:h + 1] - frow_ref[0, h:h + 1, :]))
            rr = lax.broadcasted_iota(jnp.int32, s.shape, 0)
            cc = lax.broadcasted_iota(jnp.int32, s.shape, 1)
            s = jnp.where(cc <= rr, s, NEG)
            update(h, s, vn_ref[0, :, cs])
        o_ref[0] = (acc_ref[...] / l_ref[...] * sg_ref[0]).astype(BF16)


def _attn_sample_call(q, kn, vn, ck, cv, fcol, frow, gpast, sgate, tp=512):
    b, l, aw = q.shape
    nh = aw // HEAD_DIM
    p = ck.shape[1]
    new = pl.BlockSpec((1, l, aw), lambda bi, pi: (bi, 0, 0))
    cache = pl.BlockSpec((1, tp, aw), lambda bi, pi: (bi, pi, 0))
    return pl.pallas_call(
        functools.partial(_attn_sample_body, nh=nh, scale=HEAD_DIM ** -0.5),
        grid=(b, p // tp),
        in_specs=[new, new, new, cache, cache,
                  pl.BlockSpec((1, l, nh), lambda bi, pi: (bi, 0, 0)),
                  pl.BlockSpec((1, nh, l), lambda bi, pi: (bi, 0, 0)),
                  pl.BlockSpec((1, nh, tp), lambda bi, pi: (bi, 0, pi)),
                  new],
        out_specs=new,
        out_shape=jax.ShapeDtypeStruct((b, l, aw), BF16),
        scratch_shapes=[pltpu.VMEM((l, aw), F32), pltpu.VMEM((l, aw), F32), pltpu.VMEM((l, aw), F32)],
        compiler_params=_params(("arbitrary", "arbitrary"), 40),
        name="attn_sample",
    )(q, kn, vn, ck, cv, fcol, frow, gpast, sgate)


def _pool_body(u_ref, halo_ref, hist_ref, sg_ref, w_ref, ls_ref, o_ref, ext_ref, *, pos0):
    r = pl.program_id(1)
    tm = u_ref.shape[1]
    group = u_ref.shape[2] // len(POOL_WINDOWS)
    ext_ref[HALO:HALO + tm, :] = u_ref[0]

    @pl.when(r == 0)
    def _():
        ext_ref[0:HALO, :] = hist_ref[0]

    @pl.when(r > 0)
    def _():
        ext_ref[0:HALO, :] = halo_ref[0]

    n_before = lax.broadcasted_iota(jnp.int32, (tm, 1), 0) + (pos0 + 1) + r * tm
    for gi, w in enumerate(POOL_WINDOWS):
        cs = slice(gi * group, (gi + 1) * group)
        win = ext_ref[HALO:HALO + tm, cs]
        for i in range(1, w):
            win = win + ext_ref[HALO - i:HALO - i + tm, cs]
        cnt = jnp.minimum(w, n_before).astype(F32)
        d = win / cnt - u_ref[0, :, cs]
        y = _dot(d.astype(BF16), w_ref[gi]) * ls_ref[:, cs]
        o_ref[0, :, cs] = (y * sg_ref[0, :, cs]).astype(BF16)


def _pool_call(u, hist16, sgate, w_pool, ls_pool, pos0, tm):
    b, l, bw = u.shape
    g = w_pool.shape[0]
    row = pl.BlockSpec((1, tm, bw), lambda bi, r: (bi, r, 0))
    halo_blocks = tm // HALO
    return pl.pallas_call(
        functools.partial(_pool_body, pos0=pos0),
        grid=(b, l // tm),
        in_specs=[row,
                  pl.BlockSpec((1, HALO, bw), lambda bi, r: (bi, jnp.maximum(r * halo_blocks - 1, 0), 0)),
                  pl.BlockSpec((1, HALO, bw), lambda bi, r: (bi, 0, 0)),
                  pl.BlockSpec((1, tm, bw), lambda bi, r: (bi, r, 1)),
                  pl.BlockSpec((g, bw // g, bw // g), lambda bi, r: (0, 0, 0)),
                  pl.BlockSpec((1, bw), lambda bi, r: (0, 0))],
        out_specs=row,
        out_shape=jax.ShapeDtypeStruct((b, l, bw), BF16),
        scratch_shapes=[pltpu.VMEM((HALO + tm, bw), F32)],
        compiler_params=_params(("arbitrary", "arbitrary"), 40),
        name="pool_mix",
    )(u, u, hist16, sgate, w_pool, ls_pool.reshape(1, bw))


def _sgu_body(u_ref, v_ref, gt_ref, gv_ref, bv_ref, ws_ref, bst_ref, o_ref, *vout, cl):
    lb, cw = u_ref.shape[1], u_ref.shape[2]
    gw = cw // N_SGU_GROUPS
    rr = lax.broadcasted_iota(jnp.int32, (cl, cl), 0)
    cc = lax.broadcasted_iota(jnp.int32, (cl, cl), 1)
    for c in range(lb // cl):
        rows = slice(c * cl, (c + 1) * cl)
        v = v_ref[0, rows, :]
        xc = v - jnp.mean(v, axis=-1, keepdims=True)
        var = jnp.mean(xc * xc, axis=-1, keepdims=True)
        vln = xc * lax.rsqrt(var + EPS) * gv_ref[...] + bv_ref[...]
        if vout:
            vout[0][0, rows, :] = vln
        for g in range(N_SGU_GROUPS):
            cs = slice(g * gw, (g + 1) * gw)
            ws = jnp.where(cc <= rr, ws_ref[g, :cl, :cl], 0.0).astype(BF16)
            sv = _dot(ws, vln[:, cs].astype(BF16)) + bst_ref[:cl, g:g + 1]
            o_ref[0, rows, cs] = (u_ref[0, rows, cs] * sv * gt_ref[0, rows, cs]).astype(BF16)


def _sgu_call(zact, g_v, b_v, w_s, b_s_t, lb, cl, want_v):
    b, l, cw3 = zact.shape
    cw = cw3 // 3
    out_spec = pl.BlockSpec((1, lb, cw), lambda bi, r: (bi, r, 0))
    out_shape = [jax.ShapeDtypeStruct((b, l, cw), BF16)]
    out_specs = [out_spec]
    if want_v:
        out_shape.append(jax.ShapeDtypeStruct((b, l, cw), F32))
        out_specs.append(out_spec)
    vec = pl.BlockSpec((1, cw), lambda bi, r: (0, 0))
    return pl.pallas_call(
        functools.partial(_sgu_body, cl=cl),
        grid=(b, l // lb),
        in_specs=[pl.BlockSpec((1, lb, cw), lambda bi, r: (bi, r, 0)),
                  pl.BlockSpec((1, lb, cw), lambda bi, r: (bi, r, 1)),
                  pl.BlockSpec((1, lb, cw), lambda bi, r: (bi, r, 2)),
                  vec, vec,
                  pl.BlockSpec(w_s.shape, lambda bi, r: (0, 0, 0)),
                  pl.BlockSpec(b_s_t.shape, lambda bi, r: (0, 0))],
        out_specs=out_specs,
        out_shape=out_shape,
        compiler_params=_params(("arbitrary", "arbitrary"), 48),
        name="sgu",
    )(zact, zact, zact, g_v.reshape(1, cw), b_v.reshape(1, cw), w_s, b_s_t)


def _row_blocking(b, l, rows=1024):
    if l >= rows:
        return 1, rows
    return rows // l, l


def _layer_ab(x, shift, scale, gate, g_norm, wts, hist, pos0, cache):
    b, l, d = x.shape
    aw = wts["wq"].shape[1]
    nh = aw // HEAD_DIM
    bb, lb = _row_blocking(b, l)
    h = _normmod_call(x, g_norm, scale, shift, *_row_blocking(b, l, 256)).reshape(b * l, d)

    (qn,) = _proj_call(h, wts["wq"], "headnorm", [BF16], extra=wts["gq"], name="proj_q")
    k32, k16 = _proj_call(h, wts["wk"], "headnorm", [F32, BF16], extra=wts["gk"], name="proj_k")
    v32, v16 = _proj_call(h, wts["wv"], "plain", [F32, BF16], name="proj_v")
    (logf_pad,) = _proj_call(h, wts["wf"], "logsigmoid", [F32], extra=wts["bf"], name="proj_f")
    (u,) = _proj_call(h, wts["wu"], "plain", [F32], name="proj_u")
    (sgate,) = _proj_call(h, wts["wg"], "silu", [F32], name="proj_gate")

    logf = logf_pad[:, :nh].reshape(b, l, nh)
    logf_t = logf.transpose(0, 2, 1).reshape(b * nh, l)
    fcs = _cumsum_call(logf_t, False, min(b * nh, 64))
    frow = fcs.reshape(b, nh, l)
    fcol = frow.transpose(0, 2, 1)

    q3, k3, v3 = (t.reshape(b, l, aw) for t in (qn, k16, v16))
    sgate3 = sgate.reshape(b, l, -1)
    if cache is None:
        mixed_a = _attn_prompt_call(q3, k3, v3, fcol, frow, sgate3)
    else:
        ck, cv, clogf = cache
        p = ck.shape[1]
        clf_t = clogf.transpose(0, 2, 1).reshape(b * nh, p)
        gpast = _cumsum_call(clf_t, True, 64).reshape(b, nh, p)
        mixed_a = _attn_sample_call(q3, k3, v3, ck.reshape(b, p, aw), cv.reshape(b, p, aw),
                                    fcol, frow, gpast, sgate3)

    u3 = u.reshape(b, l, -1)
    hist16 = jnp.pad(hist, ((0, 0), (HALO - POOL_HIST, 0), (0, 0)))
    mixed_b = _pool_call(u3, hist16, sgate3, wts["wpool"], wts["lspool"], pos0, min(l, 256))

    y = _outproj_call([mixed_a, mixed_b], [wts["wo_a"], wts["wo_b"]], x, gate, bb, lb)
    new_hist = jnp.concatenate([hist, u3], axis=1)[:, -POOL_HIST:]
    return (y, k32.reshape(b, l, nh, HEAD_DIM), v32.reshape(b, l, nh, HEAD_DIM), logf, new_hist)


def _layer_c(x, shift, scale, gate, g_norm, wts, want_v):
    b, l, d = x.shape
    bb, lb = _row_blocking(b, l)
    h = _normmod_call(x, g_norm, scale, shift, *_row_blocking(b, l, 256)).reshape(b * l, d)
    cw = wts["wo"].shape[0]
    tn = 512
    (zact,) = _proj_call(h, wts["win"], "gelu_silu", [F32], n_gelu_tiles=2 * cw // tn, tn=tn, name="proj_c")
    cl = min(l, SGU_CHUNK)
    outs = _sgu_call(zact.reshape(b, l, 3 * cw), wts["gv"], wts["bv"], wts["ws"], wts["bst"],
                     min(l, 2 * SGU_CHUNK), cl, want_v)
    y = _outproj_call([outs[0]], [wts["wo"]], x, gate, bb, lb)
    return y, (outs[1] if want_v else None)


def kernel(x_prompt, x_sample, cache_k, cache_v, cache_logf, state_pool, c_prompt, c_sample,
           w_ada, b_ada, g_norm, w_in_ab, b_forget, g_q, g_k, w_pool, ls_pool, w_out_ab,
           w_in_c, g_v, b_v, w_s, b_s, w_out_c):
    bp, sp, d = x_prompt.shape
    bs = x_sample.shape[0]
    depth = w_ada.shape[0]
    nh = cache_k.shape[3]
    aw = nh * HEAD_DIM
    bw = w_pool.shape[2] * w_pool.shape[1]
    past_len = cache_k.shape[2]

    c_all = jnp.concatenate([c_prompt, c_sample], axis=0)
    c_all = jnp.pad(c_all, ((0, -c_all.shape[0] % 8), (0, 0)))
    mod = _ada_call(c_all, w_ada, b_ada)

    def mods(layer, lo, n):
        m = mod[layer, lo:lo + n].reshape(n, 1, 3 * d)
        return m[..., :d], m[..., d:2 * d], m[..., 2 * d:]

    yp, ys = x_prompt, x_sample
    outs_p, outs_s, sgu_v = [], [], []
    for layer in range(depth):
        i = layer // 2
        shp, scp, gp = mods(layer, 0, bp)
        shs, scs, gs = mods(layer, bp, bs)
        if layer % 2 == 0:
            w0 = w_in_ab[i]
            o_f, o_u, o_g = 3 * aw, 3 * aw + nh, 3 * aw + nh + bw
            wts = {
                "wq": w0[:, :aw].astype(BF16),
                "wk": w0[:, aw:2 * aw].astype(BF16),
                "wv": w0[:, 2 * aw:3 * aw].astype(BF16),
                "wf": jnp.pad(w0[:, o_f:o_u], ((0, 0), (0, LANE - nh))).astype(BF16),
                "wu": w0[:, o_u:o_g].astype(BF16),
                "wg": w0[:, o_g:].astype(BF16),
                "bf": jnp.pad(b_forget[i], (0, LANE - nh)).reshape(1, LANE),
                "gq": g_q[i].reshape(1, HEAD_DIM),
                "gk": g_k[i].reshape(1, HEAD_DIM),
                "wpool": w_pool[i].astype(BF16),
                "lspool": ls_pool[i],
                "wo_a": w_out_ab[i, :aw].astype(BF16),
                "wo_b": w_out_ab[i, aw:].astype(BF16),
            }
            zero_hist = jnp.zeros((bp, POOL_HIST, bw), F32)
            rp = _layer_ab(yp, shp, scp, gp, g_norm[layer], wts, zero_hist, 0, None)
            rs = _layer_ab(ys, shs, scs, gs, g_norm[layer], wts, state_pool[i], past_len,
                           (cache_k[i], cache_v[i], cache_logf[i]))
            yp, ys = rp[0], rs[0]
            outs_p.append(rp[1:])
            outs_s.append(rs[1:])
        else:
            wts = {
                "win": w_in_c[i].astype(BF16),
                "gv": g_v[i], "bv": b_v[i],
                "ws": w_s[i], "bst": b_s[i].T,
                "wo": w_out_c[i].astype(BF16),
            }
            yp, _ = _layer_c(yp, shp, scp, gp, g_norm[layer], wts, False)
            ys, v_c = _layer_c(ys, shs, scs, gs, g_norm[layer], wts, True)
            sgu_v.append(v_c)

    def stack(group, idx):
        return jnp.stack([o[idx] for o in group])

    return (yp, ys,
            stack(outs_p, 0), stack(outs_p, 1), stack(outs_p, 2), stack(outs_p, 3),
            stack(outs_s, 0), stack(outs_s, 1), stack(outs_s, 2), stack(outs_s, 3),
            jnp.stack(sgu_v))
```

```python
import functools

import jax
import jax.numpy as jnp
from jax import lax
from jax.experimental import pallas as pl
from jax.experimental.pallas import tpu as pltpu

F32 = jnp.float32
BF16 = jnp.bfloat16

EPS = 1e-6
HEAD_DIM = 128
POOL_WINDOWS = (2, 4, 8, 16)
POOL_HIST = max(POOL_WINDOWS) - 1
HALO = POOL_HIST + 1
SGU_CHUNK = 128
N_SGU_GROUPS = 16
NEG = -1e30
LOG2E = 1.4426950408889634
LANE = 128
MIB = 1024 * 1024


def _params(sem, vmem_mib):
    return pltpu.CompilerParams(dimension_semantics=sem, vmem_limit_bytes=vmem_mib * MIB)


def _dot(a, b):
    return jnp.dot(a, b, preferred_element_type=F32)


def _dot_nt(a, b):
    return lax.dot_general(a, b, (((1,), (1,)), ((), ())), preferred_element_type=F32)


def _ada_body(c_ref, w_ref, b_ref, o_ref):
    a = jax.nn.silu(c_ref[...]).astype(BF16)
    o_ref[0] = _dot(a, w_ref[0].astype(BF16)) + b_ref[0]


def _ada_call(c_all, w_ada, b_ada, tn=512):
    depth, d, n = w_ada.shape
    rp = c_all.shape[0]
    return pl.pallas_call(
        _ada_body,
        grid=(depth, n // tn),
        in_specs=[pl.BlockSpec((rp, d), lambda l, j: (0, 0)),
                  pl.BlockSpec((1, d, tn), lambda l, j: (l, 0, j)),
                  pl.BlockSpec((1, 1, tn), lambda l, j: (l, 0, j))],
        out_specs=pl.BlockSpec((1, rp, tn), lambda l, j: (l, 0, j)),
        out_shape=jax.ShapeDtypeStruct((depth, rp, n), F32),
        compiler_params=_params(("arbitrary", "arbitrary"), 40),
        name="ada_mod",
    )(c_all, w_ada, b_ada.reshape(depth, 1, n))


def _normmod_body(x_ref, g_ref, sc_ref, sh_ref, o_ref):
    x = x_ref[...]
    y = x * lax.rsqrt(jnp.mean(x * x, axis=-1, keepdims=True) + EPS) * g_ref[...]
    o_ref[...] = (y * (1 + sc_ref[...]) + sh_ref[...]).astype(BF16)


def _normmod_call(x, g, scale, shift, bb, lb):
    b, l, d = x.shape
    row = pl.BlockSpec((bb, lb, d), lambda i, r: (i, r, 0))
    per_b = pl.BlockSpec((bb, 1, d), lambda i, r: (i, 0, 0))
    return pl.pallas_call(
        _normmod_body,
        grid=(b // bb, l // lb),
        in_specs=[row, pl.BlockSpec((1, 1, d), lambda i, r: (0, 0, 0)), per_b, per_b],
        out_specs=row,
        out_shape=jax.ShapeDtypeStruct((b, l, d), BF16),
        compiler_params=_params(("arbitrary", "arbitrary"), 40),
        name="norm_mod",
    )(x, g.reshape(1, 1, d), scale, shift)


def _proj_body(a_ref, w_ref, *rest, mode, n_gelu_tiles):
    acc = _dot(a_ref[...], w_ref[...])
    tn = acc.shape[1]
    if mode == "headnorm":
        g_ref, outs = rest[0], rest[1:]
        for hh in range(tn // HEAD_DIM):
            cs = slice(hh * HEAD_DIM, (hh + 1) * HEAD_DIM)
            blk = acc[:, cs]
            y = blk * lax.rsqrt(jnp.mean(blk * blk, axis=-1, keepdims=True) + EPS) * g_ref[...]
            for o_ref in outs:
                o_ref[:, cs] = y.astype(o_ref.dtype)
    elif mode == "plain":
        for o_ref in rest:
            o_ref[...] = acc.astype(o_ref.dtype)
    elif mode == "logsigmoid":
        b_ref, o_ref = rest
        x = acc + b_ref[...]
        o_ref[...] = jnp.minimum(x, 0.0) - jnp.log1p(jnp.exp(-jnp.abs(x)))
    elif mode == "silu":
        (o_ref,) = rest
        o_ref[...] = jax.nn.silu(acc)
    elif mode == "gelu_silu":
        (o_ref,) = rest
        j = pl.program_id(1)

        @pl.when(j < n_gelu_tiles)
        def _():
            o_ref[...] = 0.5 * acc * (1.0 + lax.erf(acc * (2.0 ** -0.5)))

        @pl.when(j >= n_gelu_tiles)
        def _():
            o_ref[...] = jax.nn.silu(acc)
    else:
        raise ValueError(mode)


def _proj_call(a, w, mode, out_dtypes, extra=None, n_gelu_tiles=0, tm=1024, tn=512, name="proj"):
    m, k = a.shape
    n = w.shape[1]
    tn = min(tn, n)
    in_specs = [pl.BlockSpec((tm, k), lambda i, j: (i, 0)),
                pl.BlockSpec((k, tn), lambda i, j: (0, j))]
    args = [a, w]
    if extra is not None:
        ew = extra.shape[1]
        if ew == n:
            in_specs.append(pl.BlockSpec((1, tn), lambda i, j: (0, j)))
        else:
            in_specs.append(pl.BlockSpec((1, ew), lambda i, j: (0, 0)))
        args.append(extra)
    out_spec = pl.BlockSpec((tm, tn), lambda i, j: (i, j))
    outs = pl.pallas_call(
        functools.partial(_proj_body, mode=mode, n_gelu_tiles=n_gelu_tiles),
        grid=(m // tm, n // tn),
        in_specs=in_specs,
        out_specs=[out_spec] * len(out_dtypes),
        out_shape=[jax.ShapeDtypeStruct((m, n), dt) for dt in out_dtypes],
        compiler_params=_params(("arbitrary", "arbitrary"), 48),
        name=name,
    )(*args)
    return outs


def _outproj_body(*refs, n_a):
    a_refs, w_refs = refs[:n_a], refs[n_a:2 * n_a]
    x_ref, g_ref, o_ref = refs[2 * n_a:]
    bb, lb, tn = x_ref.shape
    acc = None
    for a_ref, w_ref in zip(a_refs, w_refs):
        a = a_ref[...].reshape(bb * lb, a_ref.shape[2])
        d = _dot(a, w_ref[...])
        acc = d if acc is None else acc + d
    o_ref[...] = x_ref[...] + g_ref[...] * acc.reshape(bb, lb, tn)


def _outproj_call(a_list, w_list, x, gate, bb, lb, tn=512):
    b, l, n = x.shape
    n_a = len(a_list)
    in_specs = []
    for a in a_list:
        in_specs.append(pl.BlockSpec((bb, lb, a.shape[2]), lambda i, r, j: (i, r, 0)))
    for w in w_list:
        in_specs.append(pl.BlockSpec((w.shape[0], tn), lambda i, r, j: (0, j)))
    xspec = pl.BlockSpec((bb, lb, tn), lambda i, r, j: (i, r, j))
    in_specs += [xspec, pl.BlockSpec((bb, 1, tn), lambda i, r, j: (i, 0, j))]
    return pl.pallas_call(
        functools.partial(_outproj_body, n_a=n_a),
        grid=(b // bb, l // lb, n // tn),
        in_specs=in_specs,
        out_specs=xspec,
        out_shape=jax.ShapeDtypeStruct((b, l, n), F32),
        compiler_params=_params(("arbitrary", "arbitrary", "arbitrary"), 48),
        name="out_proj",
    )(*a_list, *w_list, x, gate)


def _cumsum_body(x_ref, o_ref, *, ch, reverse_exclusive):
    rb, n = x_ref.shape
    ii = lax.broadcasted_iota(jnp.int32, (ch, ch), 0)
    jj = lax.broadcasted_iota(jnp.int32, (ch, ch), 1)
    tri = (ii > jj) if reverse_exclusive else (ii <= jj)
    tri = tri.astype(F32)
    chunks = range(n // ch)
    carry = jnp.zeros((rb, 1), F32)
    for c in (reversed(chunks) if reverse_exclusive else chunks):
        xc = x_ref[:, c * ch:(c + 1) * ch]
        y = jnp.dot(xc, tri, precision=lax.Precision.HIGHEST, preferred_element_type=F32)
        o_ref[:, c * ch:(c + 1) * ch] = y + carry
        carry = carry + jnp.sum(xc, axis=-1, keepdims=True)


def _cumsum_call(x, reverse_exclusive, rb):
    r, n = x.shape
    ch = min(n, 512)
    return pl.pallas_call(
        functools.partial(_cumsum_body, ch=ch, reverse_exclusive=reverse_exclusive),
        grid=(r // rb,),
        in_specs=[pl.BlockSpec((rb, n), lambda i: (i, 0))],
        out_specs=pl.BlockSpec((rb, n), lambda i: (i, 0)),
        out_shape=jax.ShapeDtypeStruct((r, n), F32),
        compiler_params=_params(("arbitrary",), 32),
        name="cumsum",
    )(x)


def _store_scores(raw, h, c1, qb_ref, kbias, mask, s_ref, pm_ref):
    tk = raw.shape[1]
    w = min(tk, LANE)
    pm = None
    for c in range(tk // w):
        cs = slice(c * w, (c + 1) * w)
        sc = raw[:, cs] * c1 + (qb_ref[h, :, :w] + kbias[:, cs])
        if mask is not None:
            sc = jnp.where(mask[:, cs], sc, NEG)
        s_ref[h, :, cs] = sc
        pm = sc if pm is None else jnp.maximum(pm, sc)
    pm_ref[h] = pm


def _softmax_update(s_ref, p_ref, pm_ref, m_ref, a_ref):
    nslots, _, tk = s_ref.shape
    w = pm_ref.shape[2]
    for h in range(nslots):
        m_prev = m_ref[h]
        m_new = jnp.maximum(m_prev, jnp.max(pm_ref[h], axis=-1, keepdims=True))
        a_ref[h] = jnp.exp2(m_prev - m_new)
        m_ref[h] = m_new
    for h in range(nslots):
        for c in range(tk // w):
            cs = slice(c * w, (c + 1) * w)
            p_ref[h, :, cs] = jnp.exp2(s_ref[h, :, cs] - m_ref[h, :, :w]).astype(BF16)


def _accumulate(h, p, v, a_ref, l_ref, acc_ref):
    res = _dot(p, jnp.concatenate([v, jnp.ones_like(v)], axis=1))
    a = a_ref[h]
    return a * acc_ref[h] + res[:, :HEAD_DIM], a * l_ref[h] + res[:, HEAD_DIM:]


def _causal_mask(rows, cols):
    rr = lax.broadcasted_iota(jnp.int32, (rows, cols), 0)
    cc = lax.broadcasted_iota(jnp.int32, (rows, cols), 1)
    return cc <= rr


def _init_softmax_state(m_ref, l_ref, acc_ref):
    m_ref[...] = jnp.full(m_ref.shape, NEG, F32)
    l_ref[...] = jnp.zeros(l_ref.shape, F32)
    acc_ref[...] = jnp.zeros(acc_ref.shape, F32)


def _head_cols(h):
    return slice(h * HEAD_DIM, (h + 1) * HEAD_DIM)


def _attn_prompt_body(q_ref, k_ref, v_ref, fcol_ref, frow_ref, sg_ref, o_ref,
                      s_ref, p_ref, pm_ref, m_ref, l_ref, a_ref, acc_ref, qb_ref, *, tq, hg, c1):
    g = pl.program_id(1)
    qi = pl.program_id(2)
    fblk = fcol_ref[0] * LOG2E
    lane = lax.broadcasted_iota(jnp.int32, fblk.shape, 1)
    for hh in range(hg):
        fc = jnp.sum(jnp.where(lane == g * hg + hh, fblk, 0.0), axis=-1, keepdims=True)
        qb_ref[hh] = jnp.broadcast_to(fc, qb_ref.shape[1:])
    _init_softmax_state(m_ref, l_ref, acc_ref)

    def step(kj, masked):
        rows = pl.ds(pl.multiple_of(kj * tq, tq), tq)
        mask = _causal_mask(tq, tq) if masked else None
        for hh in range(hg):
            raw = _dot_nt(q_ref[0, :, _head_cols(hh)], k_ref[0, rows, _head_cols(hh)])
            kbias = frow_ref[0, kj, hh:hh + 1, :] * -LOG2E
            _store_scores(raw, hh, c1, qb_ref, kbias, mask, s_ref, pm_ref)
        _softmax_update(s_ref, p_ref, pm_ref, m_ref, a_ref)
        for hh in range(hg):
            acc_ref[hh], l_ref[hh] = _accumulate(hh, p_ref[hh], v_ref[0, rows, _head_cols(hh)],
                                                 a_ref, l_ref, acc_ref)

    def loop_body(kj, carry):
        step(kj, False)
        return carry

    lax.fori_loop(0, qi, loop_body, 0)
    step(qi, True)
    for hh in range(hg):
        cs = _head_cols(hh)
        o_ref[0, :, cs] = (acc_ref[hh] / l_ref[hh] * sg_ref[0, :, cs]).astype(BF16)


def _attn_prompt_call(q, k, v, fcol, frow, sgate, tq=512, hg=4):
    b, s, aw = q.shape
    nh = aw // HEAD_DIM
    nq = s // tq
    gw = hg * HEAD_DIM
    qspec = pl.BlockSpec((1, tq, gw), lambda bi, g, qi: (bi, qi, g))
    kvspec = pl.BlockSpec((1, s, gw), lambda bi, g, qi: (bi, 0, g))
    frow4 = frow.reshape(b * nh // hg, hg, nq, tq).transpose(0, 2, 1, 3)
    rep = pltpu.VMEM((hg, tq, LANE), F32)
    return pl.pallas_call(
        functools.partial(_attn_prompt_body, tq=tq, hg=hg, c1=HEAD_DIM ** -0.5 * LOG2E),
        grid=(b, nh // hg, nq),
        in_specs=[qspec, kvspec, kvspec,
                  pl.BlockSpec((1, tq, nh), lambda bi, g, qi: (bi, qi, 0)),
                  pl.BlockSpec((1, nq, hg, tq), lambda bi, g, qi: (bi * (nh // hg) + g, 0, 0, 0)),
                  qspec],
        out_specs=qspec,
        out_shape=jax.ShapeDtypeStruct((b, s, aw), BF16),
        scratch_shapes=[pltpu.VMEM((hg, tq, tq), F32), pltpu.VMEM((hg, tq, tq), BF16), rep,
                        rep, rep, rep, rep, rep],
        compiler_params=_params(("arbitrary", "arbitrary", "arbitrary"), 48),
        name="attn_prompt",
    )(q, k, v, fcol, frow4, sgate)


def _attn_sample_body(q_ref, kn_ref, vn_ref, ck_ref, cv_ref, fcol_ref, frow_ref, gp_ref, sg_ref, o_ref,
                      s_ref, p_ref, pm_ref, sn_ref, pn_ref, pmn_ref, m_ref, l_ref, a_ref, acc_ref,
                      qb_ref, *, nh, c1):
    pi = pl.program_id(1)
    l = q_ref.shape[1]

    @pl.when(pi == 0)
    def _():
        _init_softmax_state(m_ref, l_ref, acc_ref)
        for h in range(nh):
            qb_ref[h] = jnp.broadcast_to(fcol_ref[0, :, h:h + 1] * LOG2E, qb_ref.shape[1:])

    for h in range(nh):
        cs = _head_cols(h)
        raw = _dot_nt(q_ref[0, :, cs], ck_ref[0, :, cs].astype(BF16))
        _store_scores(raw, h, c1, qb_ref, gp_ref[0, h:h + 1, :] * LOG2E, None, s_ref, pm_ref)
    _softmax_update(s_ref, p_ref, pm_ref, m_ref, a_ref)
    for h in range(nh):
        acc_ref[h], l_ref[h] = _accumulate(h, p_ref[h], cv_ref[0, :, _head_cols(h)].astype(BF16),
                                           a_ref, l_ref, acc_ref)

    @pl.when(pi == pl.num_programs(1) - 1)
    def _():
        mask = _causal_mask(l, l)
        for h in range(nh):
            cs = _head_cols(h)
            raw = _dot_nt(q_ref[0, :, cs], kn_ref[0, :, cs])
            _store_scores(raw, h, c1, qb_ref, frow_ref[0, h:h + 1, :] * -LOG2E, mask, sn_ref, pmn_ref)
        _softmax_update(sn_ref, pn_ref, pmn_ref, m_ref, a_ref)
        for h in range(nh):
            cs = _head_cols(h)
            acc, den = _accumulate(h, pn_ref[h], vn_ref[0, :, cs], a_ref, l_ref, acc_ref)
            o_ref[0, :, cs] = (acc / den * sg_ref[0, :, cs]).astype(BF16)


def _attn_sample_call(q, kn, vn, ck, cv, cache_row0, fcol, frow, gpast, sgate, tp=512):
    b, l, aw = q.shape
    nh = aw // HEAD_DIM
    p = ck.shape[1]
    new = pl.BlockSpec((1, l, aw), lambda bi, pi: (bi, 0, 0))
    cache = pl.BlockSpec((1, tp, aw), lambda bi, pi: (bi + cache_row0, pi, 0))
    rep = pltpu.VMEM((nh, l, LANE), F32)
    lw = min(l, LANE)
    return pl.pallas_call(
        functools.partial(_attn_sample_body, nh=nh, c1=HEAD_DIM ** -0.5 * LOG2E),
        grid=(b, p // tp),
        in_specs=[new, new, new, cache, cache,
                  pl.BlockSpec((1, l, nh), lambda bi, pi: (bi, 0, 0)),
                  pl.BlockSpec((1, nh, l), lambda bi, pi: (bi, 0, 0)),
                  pl.BlockSpec((1, nh, tp), lambda bi, pi: (bi, 0, pi)),
                  new],
        out_specs=new,
        out_shape=jax.ShapeDtypeStruct((b, l, aw), BF16),
        scratch_shapes=[pltpu.VMEM((nh, l, tp), F32), pltpu.VMEM((nh, l, tp), BF16), rep,
                        pltpu.VMEM((nh, l, l), F32), pltpu.VMEM((nh, l, l), BF16),
                        pltpu.VMEM((nh, l, lw), F32),
                        rep, rep, rep, rep, rep],
        compiler_params=_params(("arbitrary", "arbitrary"), 48),
        name="attn_sample",
    )(q, kn, vn, ck, cv, fcol, frow, gpast, sgate)


def _pool_body(u_ref, halo_ref, hist_ref, sg_ref, w_ref, ls_ref, o_ref, ext_ref, *, pos0):
    r = pl.program_id(1)
    tm = u_ref.shape[1]
    group = u_ref.shape[2] // len(POOL_WINDOWS)
    ext_ref[HALO:HALO + tm, :] = u_ref[0]

    @pl.when(r == 0)
    def _():
        ext_ref[0:HALO, :] = hist_ref[0]

    @pl.when(r > 0)
    def _():
        ext_ref[0:HALO, :] = halo_ref[0]

    n_before = lax.broadcasted_iota(jnp.int32, (tm, 1), 0) + (pos0 + 1) + r * tm
    for gi, w in enumerate(POOL_WINDOWS):
        cs = slice(gi * group, (gi + 1) * group)
        win = ext_ref[HALO:HALO + tm, cs]
        for i in range(1, w):
            win = win + ext_ref[HALO - i:HALO - i + tm, cs]
        cnt = jnp.minimum(w, n_before).astype(F32)
        d = win / cnt - u_ref[0, :, cs]
        y = _dot(d.astype(BF16), w_ref[gi]) * ls_ref[:, cs]
        o_ref[0, :, cs] = (y * sg_ref[0, :, cs]).astype(BF16)


def _pool_call(u, hist16, sgate, w_pool, ls_pool, pos0, tm):
    b, l, bw = u.shape
    g = w_pool.shape[0]
    row = pl.BlockSpec((1, tm, bw), lambda bi, r: (bi, r, 0))
    halo_blocks = tm // HALO
    return pl.pallas_call(
        functools.partial(_pool_body, pos0=pos0),
        grid=(b, l // tm),
        in_specs=[row,
                  pl.BlockSpec((1, HALO, bw), lambda bi, r: (bi, jnp.maximum(r * halo_blocks - 1, 0), 0)),
                  pl.BlockSpec((1, HALO, bw), lambda bi, r: (bi, 0, 0)),
                  pl.BlockSpec((1, tm, bw), lambda bi, r: (bi, r, 1)),
                  pl.BlockSpec((g, bw // g, bw // g), lambda bi, r: (0, 0, 0)),
                  pl.BlockSpec((1, bw), lambda bi, r: (0, 0))],
        out_specs=row,
        out_shape=jax.ShapeDtypeStruct((b, l, bw), BF16),
        scratch_shapes=[pltpu.VMEM((HALO + tm, bw), F32)],
        compiler_params=_params(("arbitrary", "arbitrary"), 40),
        name="pool_mix",
    )(u, u, hist16, sgate, w_pool, ls_pool.reshape(1, bw))


def _sgu_body(u_ref, v_ref, gt_ref, gv_ref, bv_ref, ws_ref, bst_ref, o_ref, *vout, cl):
    lb, cw = u_ref.shape[1], u_ref.shape[2]
    gw = cw // N_SGU_GROUPS
    rr = lax.broadcasted_iota(jnp.int32, (cl, cl), 0)
    cc = lax.broadcasted_iota(jnp.int32, (cl, cl), 1)
    for c in range(lb // cl):
        rows = slice(c * cl, (c + 1) * cl)
        v = v_ref[0, rows, :]
        xc = v - jnp.mean(v, axis=-1, keepdims=True)
        var = jnp.mean(xc * xc, axis=-1, keepdims=True)
        vln = xc * lax.rsqrt(var + EPS) * gv_ref[...] + bv_ref[...]
        if vout:
            vout[0][0, rows, :] = vln
        for g in range(N_SGU_GROUPS):
            cs = slice(g * gw, (g + 1) * gw)
            ws = jnp.where(cc <= rr, ws_ref[g, :cl, :cl], 0.0).astype(BF16)
            sv = _dot(ws, vln[:, cs].astype(BF16)) + bst_ref[:cl, g:g + 1]
            o_ref[0, rows, cs] = (u_ref[0, rows, cs] * sv * gt_ref[0, rows, cs]).astype(BF16)


def _sgu_call(zact, g_v, b_v, w_s, b_s_t, lb, cl, want_v):
    b, l, cw3 = zact.shape
    cw = cw3 // 3
    out_spec = pl.BlockSpec((1, lb, cw), lambda bi, r: (bi, r, 0))
    out_shape = [jax.ShapeDtypeStruct((b, l, cw), BF16)]
    out_specs = [out_spec]
    if want_v:
        out_shape.append(jax.ShapeDtypeStruct((b, l, cw), F32))
        out_specs.append(out_spec)
    vec = pl.BlockSpec((1, cw), lambda bi, r: (0, 0))
    return pl.pallas_call(
        functools.partial(_sgu_body, cl=cl),
        grid=(b, l // lb),
        in_specs=[pl.BlockSpec((1, lb, cw), lambda bi, r: (bi, r, 0)),
                  pl.BlockSpec((1, lb, cw), lambda bi, r: (bi, r, 1)),
                  pl.BlockSpec((1, lb, cw), lambda bi, r: (bi, r, 2)),
                  vec, vec,
                  pl.BlockSpec(w_s.shape, lambda bi, r: (0, 0, 0)),
                  pl.BlockSpec(b_s_t.shape, lambda bi, r: (0, 0))],
        out_specs=out_specs,
        out_shape=out_shape,
        compiler_params=_params(("arbitrary", "arbitrary"), 48),
        name="sgu",
    )(zact, zact, zact, g_v.reshape(1, cw), b_v.reshape(1, cw), w_s, b_s_t)


def _row_blocking(b, l, rows=1024):
    if l >= rows:
        return 1, rows
    return rows // l, l


def _layer_ab(x, shift, scale, gate, g_norm, wts, hist, pos0, cache):
    b, l, d = x.shape
    aw = wts["wq"].shape[1]
    nh = aw // HEAD_DIM
    bb, lb = _row_blocking(b, l)
    h = _normmod_call(x, g_norm, scale, shift, *_row_blocking(b, l, 256)).reshape(b * l, d)

    (qn,) = _proj_call(h, wts["wq"], "headnorm", [BF16], extra=wts["gq"], name="proj_q")
    k32, k16 = _proj_call(h, wts["wk"], "headnorm", [F32, BF16], extra=wts["gk"], name="proj_k")
    v32, v16 = _proj_call(h, wts["wv"], "plain", [F32, BF16], name="proj_v")
    (logf_pad,) = _proj_call(h, wts["wf"], "logsigmoid", [F32], extra=wts["bf"], name="proj_f")
    (u,) = _proj_call(h, wts["wu"], "plain", [F32], name="proj_u")
    (sgate,) = _proj_call(h, wts["wg"], "silu", [F32], name="proj_gate")

    logf = logf_pad[:, :nh].reshape(b, l, nh)
    logf_t = logf.transpose(0, 2, 1).reshape(b * nh, l)
    fcs = _cumsum_call(logf_t, False, min(b * nh, 64))
    frow = fcs.reshape(b, nh, l)
    fcol = frow.transpose(0, 2, 1)

    q3, k3, v3 = (t.reshape(b, l, aw) for t in (qn, k16, v16))
    sgate3 = sgate.reshape(b, l, -1)
    if cache is None:
        mixed_a = _attn_prompt_call(q3, k3, v3, fcol, frow, sgate3)
    else:
        ck, cv, clogf, row0 = cache
        p = ck.shape[1]
        clf_t = clogf.transpose(0, 2, 1).reshape(b * nh, p)
        gpast = _cumsum_call(clf_t, True, 64).reshape(b, nh, p)
        mixed_a = _attn_sample_call(q3, k3, v3, ck, cv, row0, fcol, frow, gpast, sgate3)

    u3 = u.reshape(b, l, -1)
    hist16 = jnp.pad(hist, ((0, 0), (HALO - POOL_HIST, 0), (0, 0)))
    mixed_b = _pool_call(u3, hist16, sgate3, wts["wpool"], wts["lspool"], pos0, min(l, 256))

    y = _outproj_call([mixed_a, mixed_b], [wts["wo_a"], wts["wo_b"]], x, gate, bb, lb)
    if l >= POOL_HIST:
        new_hist = u3[:, l - POOL_HIST:]
    else:
        new_hist = jnp.concatenate([hist, u3], axis=1)[:, -POOL_HIST:]
    return (y, k32.reshape(b, l, nh, HEAD_DIM), v32.reshape(b, l, nh, HEAD_DIM), logf, new_hist)


def _layer_c(x, shift, scale, gate, g_norm, wts, want_v):
    b, l, d = x.shape
    bb, lb = _row_blocking(b, l)
    h = _normmod_call(x, g_norm, scale, shift, *_row_blocking(b, l, 256)).reshape(b * l, d)
    cw = wts["wo"].shape[0]
    tn = 512
    (zact,) = _proj_call(h, wts["win"], "gelu_silu", [F32], n_gelu_tiles=2 * cw // tn, tn=tn, name="proj_c")
    cl = min(l, SGU_CHUNK)
    outs = _sgu_call(zact.reshape(b, l, 3 * cw), wts["gv"], wts["bv"], wts["ws"], wts["bst"],
                     min(l, 2 * SGU_CHUNK), cl, want_v)
    y = _outproj_call([outs[0]], [wts["wo"]], x, gate, bb, lb)
    return y, (outs[1] if want_v else None)


def kernel(x_prompt, x_sample, cache_k, cache_v, cache_logf, state_pool, c_prompt, c_sample,
           w_ada, b_ada, g_norm, w_in_ab, b_forget, g_q, g_k, w_pool, ls_pool, w_out_ab,
           w_in_c, g_v, b_v, w_s, b_s, w_out_c):
    bp, sp, d = x_prompt.shape
    bs = x_sample.shape[0]
    depth = w_ada.shape[0]
    nh = cache_k.shape[3]
    aw = nh * HEAD_DIM
    bw = w_pool.shape[2] * w_pool.shape[1]
    past_len = cache_k.shape[2]
    ck_all = cache_k.reshape(-1, past_len, aw)
    cv_all = cache_v.reshape(-1, past_len, aw)

    c_all = jnp.concatenate([c_prompt, c_sample], axis=0)
    c_all = jnp.pad(c_all, ((0, -c_all.shape[0] % 8), (0, 0)))
    mod = _ada_call(c_all, w_ada, b_ada)

    def mods(layer, lo, n):
        m = mod[layer, lo:lo + n].reshape(n, 1, 3 * d)
        return m[..., :d], m[..., d:2 * d], m[..., 2 * d:]

    yp, ys = x_prompt, x_sample
    outs_p, outs_s, sgu_v = [], [], []
    for layer in range(depth):
        i = layer // 2
        shp, scp, gp = mods(layer, 0, bp)
        shs, scs, gs = mods(layer, bp, bs)
        if layer % 2 == 0:
            w0 = w_in_ab[i]
            o_f, o_u, o_g = 3 * aw, 3 * aw + nh, 3 * aw + nh + bw
            wts = {
                "wq": w0[:, :aw].astype(BF16),
                "wk": w0[:, aw:2 * aw].astype(BF16),
                "wv": w0[:, 2 * aw:3 * aw].astype(BF16),
                "wf": jnp.pad(w0[:, o_f:o_u], ((0, 0), (0, LANE - nh))).astype(BF16),
                "wu": w0[:, o_u:o_g].astype(BF16),
                "wg": w0[:, o_g:].astype(BF16),
                "bf": jnp.pad(b_forget[i], (0, LANE - nh)).reshape(1, LANE),
                "gq": g_q[i].reshape(1, HEAD_DIM),
                "gk": g_k[i].reshape(1, HEAD_DIM),
                "wpool": w_pool[i].astype(BF16),
                "lspool": ls_pool[i],
                "wo_a": w_out_ab[i, :aw].astype(BF16),
                "wo_b": w_out_ab[i, aw:].astype(BF16),
            }
            zero_hist = jnp.zeros((bp, POOL_HIST, bw), F32)
            rp = _layer_ab(yp, shp, scp, gp, g_norm[layer], wts, zero_hist, 0, None)
            rs = _layer_ab(ys, shs, scs, gs, g_norm[layer], wts, state_pool[i], past_len,
                           (ck_all, cv_all, cache_logf[i], i * bs))
            yp, ys = rp[0], rs[0]
            outs_p.append(rp[1:])
            outs_s.append(rs[1:])
        else:
            wts = {
                "win": w_in_c[i].astype(BF16),
                "gv": g_v[i], "bv": b_v[i],
                "ws": w_s[i], "bst": b_s[i].T,
                "wo": w_out_c[i].astype(BF16),
            }
            yp, _ = _layer_c(yp, shp, scp, gp, g_norm[layer], wts, False)
            ys, v_c = _layer_c(ys, shs, scs, gs, g_norm[layer], wts, True)
            sgu_v.append(v_c)

    def stack(group, idx):
        return jnp.stack([o[idx] for o in group])

    return (yp, ys,
            stack(outs_p, 0), stack(outs_p, 1), stack(outs_p, 2), stack(outs_p, 3),
            stack(outs_s, 0), stack(outs_s, 1), stack(outs_s, 2), stack(outs_s, 3),
            jnp.stack(sgu_v))
```

```python
import functools

import jax
import jax.numpy as jnp
from jax import lax
from jax.experimental import pallas as pl
from jax.experimental.pallas import tpu as pltpu

F32 = jnp.float32
BF16 = jnp.bfloat16

EPS = 1e-6
HEAD_DIM = 128
POOL_WINDOWS = (2, 4, 8, 16)
POOL_HIST = max(POOL_WINDOWS) - 1
HALO = POOL_HIST + 1
SGU_CHUNK = 128
N_SGU_GROUPS = 16
NEG = -1e30
LOG2E = 1.4426950408889634
LANE = 128
MIB = 1024 * 1024


def _params(sem, vmem_mib):
    return pltpu.CompilerParams(dimension_semantics=sem, vmem_limit_bytes=vmem_mib * MIB)


def _dot(a, b):
    return jnp.dot(a, b, preferred_element_type=F32)


def _dot_nt(a, b):
    return lax.dot_general(a, b, (((1,), (1,)), ((), ())), preferred_element_type=F32)


def _ada_body(c_ref, w_ref, b_ref, o_ref):
    a = jax.nn.silu(c_ref[...]).astype(BF16)
    o_ref[0] = _dot(a, w_ref[0].astype(BF16)) + b_ref[0]


def _ada_call(c_all, w_ada, b_ada, tn=512):
    depth, d, n = w_ada.shape
    rp = c_all.shape[0]
    return pl.pallas_call(
        _ada_body,
        grid=(depth, n // tn),
        in_specs=[pl.BlockSpec((rp, d), lambda l, j: (0, 0)),
                  pl.BlockSpec((1, d, tn), lambda l, j: (l, 0, j)),
                  pl.BlockSpec((1, 1, tn), lambda l, j: (l, 0, j))],
        out_specs=pl.BlockSpec((1, rp, tn), lambda l, j: (l, 0, j)),
        out_shape=jax.ShapeDtypeStruct((depth, rp, n), F32),
        compiler_params=_params(("arbitrary", "arbitrary"), 40),
        name="ada_mod",
    )(c_all, w_ada, b_ada.reshape(depth, 1, n))


def _normmod_body(x_ref, g_ref, sc_ref, sh_ref, o_ref):
    x = x_ref[...]
    y = x * lax.rsqrt(jnp.mean(x * x, axis=-1, keepdims=True) + EPS) * g_ref[...]
    o_ref[...] = (y * (1 + sc_ref[...]) + sh_ref[...]).astype(BF16)


def _normmod_call(x, g, scale, shift, bb, lb):
    b, l, d = x.shape
    row = pl.BlockSpec((bb, lb, d), lambda i, r: (i, r, 0))
    per_b = pl.BlockSpec((bb, 1, d), lambda i, r: (i, 0, 0))
    return pl.pallas_call(
        _normmod_body,
        grid=(b // bb, l // lb),
        in_specs=[row, pl.BlockSpec((1, 1, d), lambda i, r: (0, 0, 0)), per_b, per_b],
        out_specs=row,
        out_shape=jax.ShapeDtypeStruct((b, l, d), BF16),
        compiler_params=_params(("arbitrary", "arbitrary"), 40),
        name="norm_mod",
    )(x, g.reshape(1, 1, d), scale, shift)


def _proj_body(a_ref, w_ref, *rest, mode, n_gelu_tiles):
    acc = _dot(a_ref[...], w_ref[...])
    tn = acc.shape[1]
    if mode == "headnorm":
        g_ref, outs = rest[0], rest[1:]
        for hh in range(tn // HEAD_DIM):
            cs = slice(hh * HEAD_DIM, (hh + 1) * HEAD_DIM)
            blk = acc[:, cs]
            y = blk * lax.rsqrt(jnp.mean(blk * blk, axis=-1, keepdims=True) + EPS) * g_ref[...]
            for o_ref in outs:
                o_ref[:, cs] = y.astype(o_ref.dtype)
    elif mode == "plain":
        for o_ref in rest:
            o_ref[...] = acc.astype(o_ref.dtype)
    elif mode == "logsigmoid":
        b_ref, o_ref = rest
        x = acc + b_ref[...]
        o_ref[...] = jnp.minimum(x, 0.0) - jnp.log1p(jnp.exp(-jnp.abs(x)))
    elif mode == "silu":
        (o_ref,) = rest
        o_ref[...] = jax.nn.silu(acc).astype(o_ref.dtype)
    elif mode == "gelu_silu":
        (o_ref,) = rest
        j = pl.program_id(1)

        @pl.when(j < n_gelu_tiles)
        def _():
            o_ref[...] = (0.5 * acc * (1.0 + lax.erf(acc * (2.0 ** -0.5)))).astype(o_ref.dtype)

        @pl.when(j >= n_gelu_tiles)
        def _():
            o_ref[...] = jax.nn.silu(acc).astype(o_ref.dtype)
    else:
        raise ValueError(mode)


def _proj_call(a, w, mode, out_dtypes, extra=None, n_gelu_tiles=0, tm=1024, tn=512, name="proj"):
    m, k = a.shape
    n = w.shape[1]
    tn = min(tn, n)
    in_specs = [pl.BlockSpec((tm, k), lambda i, j: (i, 0)),
                pl.BlockSpec((k, tn), lambda i, j: (0, j))]
    args = [a, w]
    if extra is not None:
        ew = extra.shape[1]
        if ew == n:
            in_specs.append(pl.BlockSpec((1, tn), lambda i, j: (0, j)))
        else:
            in_specs.append(pl.BlockSpec((1, ew), lambda i, j: (0, 0)))
        args.append(extra)
    out_spec = pl.BlockSpec((tm, tn), lambda i, j: (i, j))
    outs = pl.pallas_call(
        functools.partial(_proj_body, mode=mode, n_gelu_tiles=n_gelu_tiles),
        grid=(m // tm, n // tn),
        in_specs=in_specs,
        out_specs=[out_spec] * len(out_dtypes),
        out_shape=[jax.ShapeDtypeStruct((m, n), dt) for dt in out_dtypes],
        compiler_params=_params(("arbitrary", "arbitrary"), 48),
        name=name,
    )(*args)
    return outs


def _outproj_body(*refs, n_a):
    a_refs, w_refs = refs[:n_a], refs[n_a:2 * n_a]
    x_ref, g_ref, o_ref = refs[2 * n_a:]
    bb, lb, tn = x_ref.shape
    acc = None
    for a_ref, w_ref in zip(a_refs, w_refs):
        a = a_ref[...].reshape(bb * lb, a_ref.shape[2])
        d = _dot(a, w_ref[...])
        acc = d if acc is None else acc + d
    o_ref[...] = x_ref[...] + g_ref[...] * acc.reshape(bb, lb, tn)


def _outproj_call(a_list, w_list, x, gate, bb, lb, tn=512):
    b, l, n = x.shape
    n_a = len(a_list)
    in_specs = []
    for a in a_list:
        in_specs.append(pl.BlockSpec((bb, lb, a.shape[2]), lambda i, r, j: (i, r, 0)))
    for w in w_list:
        in_specs.append(pl.BlockSpec((w.shape[0], tn), lambda i, r, j: (0, j)))
    xspec = pl.BlockSpec((bb, lb, tn), lambda i, r, j: (i, r, j))
    in_specs += [xspec, pl.BlockSpec((bb, 1, tn), lambda i, r, j: (i, 0, j))]
    return pl.pallas_call(
        functools.partial(_outproj_body, n_a=n_a),
        grid=(b // bb, l // lb, n // tn),
        in_specs=in_specs,
        out_specs=xspec,
        out_shape=jax.ShapeDtypeStruct((b, l, n), F32),
        compiler_params=_params(("arbitrary", "arbitrary", "arbitrary"), 48),
        name="out_proj",
    )(*a_list, *w_list, x, gate)


def _cumsum_body(x_ref, o_ref, *, ch, reverse_exclusive):
    rb, n = x_ref.shape
    ii = lax.broadcasted_iota(jnp.int32, (ch, ch), 0)
    jj = lax.broadcasted_iota(jnp.int32, (ch, ch), 1)
    tri = (ii > jj) if reverse_exclusive else (ii <= jj)
    tri = tri.astype(F32)
    chunks = range(n // ch)
    carry = jnp.zeros((rb, 1), F32)
    for c in (reversed(chunks) if reverse_exclusive else chunks):
        xc = x_ref[:, c * ch:(c + 1) * ch]
        y = jnp.dot(xc, tri, precision=lax.Precision.HIGHEST, preferred_element_type=F32)
        o_ref[:, c * ch:(c + 1) * ch] = y + carry
        carry = carry + jnp.sum(xc, axis=-1, keepdims=True)


def _cumsum_call(x, reverse_exclusive, rb):
    r, n = x.shape
    ch = min(n, 512)
    return pl.pallas_call(
        functools.partial(_cumsum_body, ch=ch, reverse_exclusive=reverse_exclusive),
        grid=(r // rb,),
        in_specs=[pl.BlockSpec((rb, n), lambda i: (i, 0))],
        out_specs=pl.BlockSpec((rb, n), lambda i: (i, 0)),
        out_shape=jax.ShapeDtypeStruct((r, n), F32),
        compiler_params=_params(("arbitrary",), 32),
        name="cumsum",
    )(x)


def _store_scores(raw, h, c1, qb_ref, kbias, mask, s_ref, pm_ref):
    tk = raw.shape[1]
    w = min(tk, LANE)
    pm = None
    for c in range(tk // w):
        cs = slice(c * w, (c + 1) * w)
        sc = raw[:, cs] * c1 + (qb_ref[h, :, :w] + kbias[:, cs])
        if mask is not None:
            sc = jnp.where(mask[:, cs], sc, NEG)
        s_ref[h, :, cs] = sc
        pm = sc if pm is None else jnp.maximum(pm, sc)
    pm_ref[h] = pm


def _softmax_update(s_ref, p_ref, pm_ref, m_ref, a_ref):
    nslots, _, tk = s_ref.shape
    w = pm_ref.shape[2]
    for h in range(nslots):
        m_prev = m_ref[h]
        m_new = jnp.maximum(m_prev, jnp.max(pm_ref[h], axis=-1, keepdims=True))
        a_ref[h] = jnp.exp2(m_prev - m_new)
        m_ref[h] = m_new
    for h in range(nslots):
        for c in range(tk // w):
            cs = slice(c * w, (c + 1) * w)
            p_ref[h, :, cs] = jnp.exp2(s_ref[h, :, cs] - m_ref[h, :, :w]).astype(BF16)


def _accumulate(h, p, v, a_ref, l_ref, acc_ref):
    res = _dot(p, jnp.concatenate([v, jnp.ones_like(v)], axis=1))
    a = a_ref[h]
    return a * acc_ref[h] + res[:, :HEAD_DIM], a * l_ref[h] + res[:, HEAD_DIM:]


def _causal_mask(rows, cols):
    rr = lax.broadcasted_iota(jnp.int32, (rows, cols), 0)
    cc = lax.broadcasted_iota(jnp.int32, (rows, cols), 1)
    return cc <= rr


def _init_softmax_state(m_ref, l_ref, acc_ref):
    m_ref[...] = jnp.full(m_ref.shape, NEG, F32)
    l_ref[...] = jnp.zeros(l_ref.shape, F32)
    acc_ref[...] = jnp.zeros(acc_ref.shape, F32)


def _head_cols(h):
    return slice(h * HEAD_DIM, (h + 1) * HEAD_DIM)


def _attn_prompt_body(q_ref, k_ref, v_ref, fcol_ref, frow_ref, sg_ref, o_ref,
                      s_ref, p_ref, pm_ref, m_ref, l_ref, a_ref, acc_ref, qb_ref, *, tq, hg, c1):
    g = pl.program_id(1)
    qi = pl.program_id(2)
    fblk = fcol_ref[0] * LOG2E
    lane = lax.broadcasted_iota(jnp.int32, fblk.shape, 1)
    for hh in range(hg):
        fc = jnp.sum(jnp.where(lane == g * hg + hh, fblk, 0.0), axis=-1, keepdims=True)
        qb_ref[hh] = jnp.broadcast_to(fc, qb_ref.shape[1:])
    _init_softmax_state(m_ref, l_ref, acc_ref)

    def step(kj, masked):
        rows = pl.ds(pl.multiple_of(kj * tq, tq), tq)
        mask = _causal_mask(tq, tq) if masked else None
        for hh in range(hg):
            raw = _dot_nt(q_ref[0, :, _head_cols(hh)], k_ref[0, rows, _head_cols(hh)])
            kbias = frow_ref[0, kj, hh:hh + 1, :] * -LOG2E
            _store_scores(raw, hh, c1, qb_ref, kbias, mask, s_ref, pm_ref)
        _softmax_update(s_ref, p_ref, pm_ref, m_ref, a_ref)
        for hh in range(hg):
            acc_ref[hh], l_ref[hh] = _accumulate(hh, p_ref[hh], v_ref[0, rows, _head_cols(hh)],
                                                 a_ref, l_ref, acc_ref)

    def loop_body(kj, carry):
        step(kj, False)
        return carry

    lax.fori_loop(0, qi, loop_body, 0)
    step(qi, True)
    for hh in range(hg):
        cs = _head_cols(hh)
        o_ref[0, :, cs] = (acc_ref[hh] / l_ref[hh] * sg_ref[0, :, cs]).astype(BF16)


def _attn_prompt_call(q, k, v, fcol, frow, sgate, tq=512, hg=4):
    b, s, aw = q.shape
    nh = aw // HEAD_DIM
    nq = s // tq
    gw = hg * HEAD_DIM
    qspec = pl.BlockSpec((1, tq, gw), lambda bi, g, qi: (bi, qi, g))
    kvspec = pl.BlockSpec((1, s, gw), lambda bi, g, qi: (bi, 0, g))
    frow4 = frow.reshape(b * nh // hg, hg, nq, tq).transpose(0, 2, 1, 3)
    rep = pltpu.VMEM((hg, tq, LANE), F32)
    return pl.pallas_call(
        functools.partial(_attn_prompt_body, tq=tq, hg=hg, c1=HEAD_DIM ** -0.5 * LOG2E),
        grid=(b, nh // hg, nq),
        in_specs=[qspec, kvspec, kvspec,
                  pl.BlockSpec((1, tq, nh), lambda bi, g, qi: (bi, qi, 0)),
                  pl.BlockSpec((1, nq, hg, tq), lambda bi, g, qi: (bi * (nh // hg) + g, 0, 0, 0)),
                  qspec],
        out_specs=qspec,
        out_shape=jax.ShapeDtypeStruct((b, s, aw), BF16),
        scratch_shapes=[pltpu.VMEM((hg, tq, tq), F32), pltpu.VMEM((hg, tq, tq), BF16), rep,
                        rep, rep, rep, rep, rep],
        compiler_params=_params(("arbitrary", "arbitrary", "arbitrary"), 48),
        name="attn_prompt",
    )(q, k, v, fcol, frow4, sgate)


def _attn_sample_body(q_ref, kn_ref, vn_ref, ck_ref, cv_ref, fcol_ref, frow_ref, gp_ref, sg_ref, o_ref,
                      s_ref, p_ref, pm_ref, sn_ref, pn_ref, pmn_ref, m_ref, l_ref, a_ref, acc_ref,
                      qb_ref, *, nh, c1):
    pi = pl.program_id(1)
    l = q_ref.shape[1]
    tp = s_ref.shape[2]

    @pl.when(pi == 0)
    def _():
        _init_softmax_state(m_ref, l_ref, acc_ref)
        for h in range(nh):
            qb_ref[h] = jnp.broadcast_to(fcol_ref[0, :, h:h + 1] * LOG2E, qb_ref.shape[1:])

    for h in range(nh):
        cs = _head_cols(h)
        raw = _dot_nt(q_ref[0, :, cs], ck_ref[0, pl.ds(h, tp, stride=nh), :].astype(BF16))
        _store_scores(raw, h, c1, qb_ref, gp_ref[0, h:h + 1, :] * LOG2E, None, s_ref, pm_ref)
    _softmax_update(s_ref, p_ref, pm_ref, m_ref, a_ref)
    for h in range(nh):
        acc_ref[h], l_ref[h] = _accumulate(h, p_ref[h], cv_ref[0, pl.ds(h, tp, stride=nh), :].astype(BF16),
                                           a_ref, l_ref, acc_ref)

    @pl.when(pi == pl.num_programs(1) - 1)
    def _():
        mask = _causal_mask(l, l)
        for h in range(nh):
            cs = _head_cols(h)
            raw = _dot_nt(q_ref[0, :, cs], kn_ref[0, :, cs])
            _store_scores(raw, h, c1, qb_ref, frow_ref[0, h:h + 1, :] * -LOG2E, mask, sn_ref, pmn_ref)
        _softmax_update(sn_ref, pn_ref, pmn_ref, m_ref, a_ref)
        for h in range(nh):
            cs = _head_cols(h)
            acc, den = _accumulate(h, pn_ref[h], vn_ref[0, :, cs], a_ref, l_ref, acc_ref)
            o_ref[0, :, cs] = (acc / den * sg_ref[0, :, cs]).astype(BF16)


def _attn_sample_call(q, kn, vn, ck, cv, layer, fcol, frow, gpast, sgate, tp=512):
    b, l, aw = q.shape
    nh = aw // HEAD_DIM
    p = ck.shape[2]
    new = pl.BlockSpec((1, l, aw), lambda bi, pi: (bi, 0, 0))
    cache = pl.BlockSpec((1, tp * nh, HEAD_DIM), lambda bi, pi: (layer * b + bi, pi, 0))
    rep = pltpu.VMEM((nh, l, LANE), F32)
    lw = min(l, LANE)
    return pl.pallas_call(
        functools.partial(_attn_sample_body, nh=nh, c1=HEAD_DIM ** -0.5 * LOG2E),
        grid=(b, p // tp),
        in_specs=[new, new, new, cache, cache,
                  pl.BlockSpec((1, l, nh), lambda bi, pi: (bi, 0, 0)),
                  pl.BlockSpec((1, nh, l), lambda bi, pi: (bi, 0, 0)),
                  pl.BlockSpec((1, nh, tp), lambda bi, pi: (bi, 0, pi)),
                  new],
        out_specs=new,
        out_shape=jax.ShapeDtypeStruct((b, l, aw), BF16),
        scratch_shapes=[pltpu.VMEM((nh, l, tp), F32), pltpu.VMEM((nh, l, tp), BF16), rep,
                        pltpu.VMEM((nh, l, l), F32), pltpu.VMEM((nh, l, l), BF16),
                        pltpu.VMEM((nh, l, lw), F32),
                        rep, rep, rep, rep, rep],
        compiler_params=_params(("arbitrary", "arbitrary"), 48),
        name="attn_sample",
    )(q, kn, vn, ck.reshape(-1, p * nh, HEAD_DIM), cv.reshape(-1, p * nh, HEAD_DIM), fcol, frow, gpast, sgate)


def _pool_body(u_ref, halo_ref, hist_ref, sg_ref, w_ref, ls_ref, o_ref, ext_ref, *, pos0):
    r = pl.program_id(1)
    tm = u_ref.shape[1]
    group = u_ref.shape[2] // len(POOL_WINDOWS)
    ext_ref[HALO:HALO + tm, :] = u_ref[0]

    @pl.when(r == 0)
    def _():
        ext_ref[0:HALO, :] = hist_ref[0]

    @pl.when(r > 0)
    def _():
        ext_ref[0:HALO, :] = halo_ref[0]

    n_before = lax.broadcasted_iota(jnp.int32, (tm, 1), 0) + (pos0 + 1) + r * tm
    for gi, w in enumerate(POOL_WINDOWS):
        cs = slice(gi * group, (gi + 1) * group)
        win = ext_ref[HALO:HALO + tm, cs]
        for i in range(1, w):
            win = win + ext_ref[HALO - i:HALO - i + tm, cs]
        cnt = jnp.minimum(w, n_before).astype(F32)
        d = win / cnt - u_ref[0, :, cs]
        y = _dot(d.astype(BF16), w_ref[gi]) * ls_ref[:, cs]
        o_ref[0, :, cs] = (y * sg_ref[0, :, cs]).astype(BF16)


def _pool_call(u, hist16, sgate, w_pool, ls_pool, pos0, tm):
    b, l, bw = u.shape
    g = w_pool.shape[0]
    row = pl.BlockSpec((1, tm, bw), lambda bi, r: (bi, r, 0))
    halo_blocks = tm // HALO
    return pl.pallas_call(
        functools.partial(_pool_body, pos0=pos0),
        grid=(b, l // tm),
        in_specs=[row,
                  pl.BlockSpec((1, HALO, bw), lambda bi, r: (bi, jnp.maximum(r * halo_blocks - 1, 0), 0)),
                  pl.BlockSpec((1, HALO, bw), lambda bi, r: (bi, 0, 0)),
                  pl.BlockSpec((1, tm, bw), lambda bi, r: (bi, r, 1)),
                  pl.BlockSpec((g, bw // g, bw // g), lambda bi, r: (0, 0, 0)),
                  pl.BlockSpec((1, bw), lambda bi, r: (0, 0))],
        out_specs=row,
        out_shape=jax.ShapeDtypeStruct((b, l, bw), BF16),
        scratch_shapes=[pltpu.VMEM((HALO + tm, bw), F32)],
        compiler_params=_params(("arbitrary", "arbitrary"), 40),
        name="pool_mix",
    )(u, u, hist16, sgate, w_pool, ls_pool.reshape(1, bw))


def _sgu_body(u_ref, v_ref, gt_ref, gv_ref, bv_ref, ws_ref, bst_ref, o_ref, *vout, cl):
    lb, cw = u_ref.shape[1], u_ref.shape[2]
    gw = cw // N_SGU_GROUPS
    rr = lax.broadcasted_iota(jnp.int32, (cl, cl), 0)
    cc = lax.broadcasted_iota(jnp.int32, (cl, cl), 1)
    for c in range(lb // cl):
        rows = slice(c * cl, (c + 1) * cl)
        v = v_ref[0, rows, :].astype(F32)
        xc = v - jnp.mean(v, axis=-1, keepdims=True)
        var = jnp.mean(xc * xc, axis=-1, keepdims=True)
        vln = xc * lax.rsqrt(var + EPS) * gv_ref[...] + bv_ref[...]
        if vout:
            vout[0][0, rows, :] = vln
        for g in range(N_SGU_GROUPS):
            cs = slice(g * gw, (g + 1) * gw)
            ws = jnp.where(cc <= rr, ws_ref[g, :cl, :cl], 0.0).astype(BF16)
            sv = _dot(ws, vln[:, cs].astype(BF16)) + bst_ref[:cl, g:g + 1]
            o_ref[0, rows, cs] = (u_ref[0, rows, cs] * sv * gt_ref[0, rows, cs]).astype(BF16)


def _sgu_call(zact, g_v, b_v, w_s, b_s_t, lb, cl, want_v):
    b, l, cw3 = zact.shape
    cw = cw3 // 3
    out_spec = pl.BlockSpec((1, lb, cw), lambda bi, r: (bi, r, 0))
    out_shape = [jax.ShapeDtypeStruct((b, l, cw), BF16)]
    out_specs = [out_spec]
    if want_v:
        out_shape.append(jax.ShapeDtypeStruct((b, l, cw), F32))
        out_specs.append(out_spec)
    vec = pl.BlockSpec((1, cw), lambda bi, r: (0, 0))
    return pl.pallas_call(
        functools.partial(_sgu_body, cl=cl),
        grid=(b, l // lb),
        in_specs=[pl.BlockSpec((1, lb, cw), lambda bi, r: (bi, r, 0)),
                  pl.BlockSpec((1, lb, cw), lambda bi, r: (bi, r, 1)),
                  pl.BlockSpec((1, lb, cw), lambda bi, r: (bi, r, 2)),
                  vec, vec,
                  pl.BlockSpec(w_s.shape, lambda bi, r: (0, 0, 0)),
                  pl.BlockSpec(b_s_t.shape, lambda bi, r: (0, 0))],
        out_specs=out_specs,
        out_shape=out_shape,
        compiler_params=_params(("arbitrary", "arbitrary"), 48),
        name="sgu",
    )(zact, zact, zact, g_v.reshape(1, cw), b_v.reshape(1, cw), w_s, b_s_t)


def _row_blocking(b, l, rows=1024):
    if l >= rows:
        return 1, rows
    return rows // l, l


def _layer_ab(x, shift, scale, gate, g_norm, wts, hist, pos0, cache):
    b, l, d = x.shape
    aw = wts["wq"].shape[1]
    nh = aw // HEAD_DIM
    bb, lb = _row_blocking(b, l)
    h = _normmod_call(x, g_norm, scale, shift, *_row_blocking(b, l, 256)).reshape(b * l, d)

    (qn,) = _proj_call(h, wts["wq"], "headnorm", [BF16], extra=wts["gq"], name="proj_q")
    k32, k16 = _proj_call(h, wts["wk"], "headnorm", [F32, BF16], extra=wts["gk"], name="proj_k")
    v32, v16 = _proj_call(h, wts["wv"], "plain", [F32, BF16], name="proj_v")
    (logf_pad,) = _proj_call(h, wts["wf"], "logsigmoid", [F32], extra=wts["bf"], name="proj_f")
    (u,) = _proj_call(h, wts["wu"], "plain", [F32], name="proj_u")
    (sgate,) = _proj_call(h, wts["wg"], "silu", [BF16], name="proj_gate")

    logf = logf_pad[:, :nh].reshape(b, l, nh)
    logf_t = logf.transpose(0, 2, 1).reshape(b * nh, l)
    fcs = _cumsum_call(logf_t, False, min(b * nh, 64))
    frow = fcs.reshape(b, nh, l)
    fcol = frow.transpose(0, 2, 1)

    q3, k3, v3 = (t.reshape(b, l, aw) for t in (qn, k16, v16))
    sgate3 = sgate.reshape(b, l, -1)
    if cache is None:
        mixed_a = _attn_prompt_call(q3, k3, v3, fcol, frow, sgate3)
    else:
        ck, cv, clogf, layer = cache
        p = ck.shape[2]
        clf_t = clogf.transpose(0, 2, 1).reshape(b * nh, p)
        gpast = _cumsum_call(clf_t, True, 64).reshape(b, nh, p)
        mixed_a = _attn_sample_call(q3, k3, v3, ck, cv, layer, fcol, frow, gpast, sgate3)

    u3 = u.reshape(b, l, -1)
    hist16 = jnp.pad(hist, ((0, 0), (HALO - POOL_HIST, 0), (0, 0)))
    mixed_b = _pool_call(u3, hist16, sgate3, wts["wpool"], wts["lspool"], pos0, min(l, 256))

    y = _outproj_call([mixed_a, mixed_b], [wts["wo_a"], wts["wo_b"]], x, gate, bb, lb)
    if l >= POOL_HIST:
        new_hist = u3[:, l - POOL_HIST:]
    else:
        new_hist = jnp.concatenate([hist, u3], axis=1)[:, -POOL_HIST:]
    return (y, k32.reshape(b, l, nh, HEAD_DIM), v32.reshape(b, l, nh, HEAD_DIM), logf, new_hist)


def _layer_c(x, shift, scale, gate, g_norm, wts, want_v):
    b, l, d = x.shape
    bb, lb = _row_blocking(b, l)
    h = _normmod_call(x, g_norm, scale, shift, *_row_blocking(b, l, 256)).reshape(b * l, d)
    cw = wts["wo"].shape[0]
    tn = 512
    (zact,) = _proj_call(h, wts["win"], "gelu_silu", [BF16], n_gelu_tiles=2 * cw // tn, tn=tn, name="proj_c")
    cl = min(l, SGU_CHUNK)
    outs = _sgu_call(zact.reshape(b, l, 3 * cw), wts["gv"], wts["bv"], wts["ws"], wts["bst"],
                     min(l, 2 * SGU_CHUNK), cl, want_v)
    y = _outproj_call([outs[0]], [wts["wo"]], x, gate, bb, lb)
    return y, (outs[1] if want_v else None)


def kernel(x_prompt, x_sample, cache_k, cache_v, cache_logf, state_pool, c_prompt, c_sample,
           w_ada, b_ada, g_norm, w_in_ab, b_forget, g_q, g_k, w_pool, ls_pool, w_out_ab,
           w_in_c, g_v, b_v, w_s, b_s, w_out_c):
    bp, sp, d = x_prompt.shape
    bs = x_sample.shape[0]
    depth = w_ada.shape[0]
    nh = cache_k.shape[3]
    aw = nh * HEAD_DIM
    bw = w_pool.shape[2] * w_pool.shape[1]
    past_len = cache_k.shape[2]

    c_all = jnp.concatenate([c_prompt, c_sample], axis=0)
    c_all = jnp.pad(c_all, ((0, -c_all.shape[0] % 8), (0, 0)))
    mod = _ada_call(c_all, w_ada, b_ada)

    def mods(layer, lo, n):
        m = mod[layer, lo:lo + n].reshape(n, 1, 3 * d)
        return m[..., :d], m[..., d:2 * d], m[..., 2 * d:]

    yp, ys = x_prompt, x_sample
    outs_p, outs_s, sgu_v = [], [], []
    for layer in range(depth):
        i = layer // 2
        shp, scp, gp = mods(layer, 0, bp)
        shs, scs, gs = mods(layer, bp, bs)
        if layer % 2 == 0:
            w0 = w_in_ab[i]
            o_f, o_u, o_g = 3 * aw, 3 * aw + nh, 3 * aw + nh + bw
            wts = {
                "wq": w0[:, :aw].astype(BF16),
                "wk": w0[:, aw:2 * aw].astype(BF16),
                "wv": w0[:, 2 * aw:3 * aw].astype(BF16),
                "wf": jnp.pad(w0[:, o_f:o_u], ((0, 0), (0, LANE - nh))).astype(BF16),
                "wu": w0[:, o_u:o_g].astype(BF16),
                "wg": w0[:, o_g:].astype(BF16),
                "bf": jnp.pad(b_forget[i], (0, LANE - nh)).reshape(1, LANE),
                "gq": g_q[i].reshape(1, HEAD_DIM),
                "gk": g_k[i].reshape(1, HEAD_DIM),
                "wpool": w_pool[i].astype(BF16),
                "lspool": ls_pool[i],
                "wo_a": w_out_ab[i, :aw].astype(BF16),
                "wo_b": w_out_ab[i, aw:].astype(BF16),
            }
            zero_hist = jnp.zeros((bp, POOL_HIST, bw), F32)
            rp = _layer_ab(yp, shp, scp, gp, g_norm[layer], wts, zero_hist, 0, None)
            rs = _layer_ab(ys, shs, scs, gs, g_norm[layer], wts, state_pool[i], past_len,
                           (cache_k, cache_v, cache_logf[i], i))
            yp, ys = rp[0], rs[0]
            outs_p.append(rp[1:])
            outs_s.append(rs[1:])
        else:
            wts = {
                "win": w_in_c[i].astype(BF16),
                "gv": g_v[i], "bv": b_v[i],
                "ws": w_s[i], "bst": b_s[i].T,
                "wo": w_out_c[i].astype(BF16),
            }
            yp, _ = _layer_c(yp, shp, scp, gp, g_norm[layer], wts, False)
            ys, v_c = _layer_c(ys, shs, scs, gs, g_norm[layer], wts, True)
            sgu_v.append(v_c)

    def stack(group, idx):
        return jnp.stack([o[idx] for o in group])

    return (yp, ys,
            stack(outs_p, 0), stack(outs_p, 1), stack(outs_p, 2), stack(outs_p, 3),
            stack(outs_s, 0), stack(outs_s, 1), stack(outs_s, 2), stack(outs_s, 3),
            jnp.stack(sgu_v))
```

```python
import functools

import jax
import jax.numpy as jnp
from jax import lax
from jax.experimental import pallas as pl
from jax.experimental.pallas import tpu as pltpu

F32 = jnp.float32
BF16 = jnp.bfloat16

EPS = 1e-6
HEAD_DIM = 128
POOL_WINDOWS = (2, 4, 8, 16)
POOL_HIST = max(POOL_WINDOWS) - 1
HALO = POOL_HIST + 1
SGU_CHUNK = 128
N_SGU_GROUPS = 16
NEG = -1e30
LOG2E = 1.4426950408889634
LANE = 128
MIB = 1024 * 1024


def _params(sem, vmem_mib):
    return pltpu.CompilerParams(dimension_semantics=sem, vmem_limit_bytes=vmem_mib * MIB)


def _dot(a, b):
    return jnp.dot(a, b, preferred_element_type=F32)


def _dot_nt(a, b):
    return lax.dot_general(a, b, (((1,), (1,)), ((), ())), preferred_element_type=F32)


def _ada_body(c_ref, w_ref, b_ref, o_ref):
    a = jax.nn.silu(c_ref[...]).astype(BF16)
    o_ref[0] = _dot(a, w_ref[0].astype(BF16)) + b_ref[0]


def _ada_call(c_all, w_ada, b_ada, tn=512):
    depth, d, n = w_ada.shape
    rp = c_all.shape[0]
    return pl.pallas_call(
        _ada_body,
        grid=(depth, n // tn),
        in_specs=[pl.BlockSpec((rp, d), lambda l, j: (0, 0)),
                  pl.BlockSpec((1, d, tn), lambda l, j: (l, 0, j)),
                  pl.BlockSpec((1, 1, tn), lambda l, j: (l, 0, j))],
        out_specs=pl.BlockSpec((1, rp, tn), lambda l, j: (l, 0, j)),
        out_shape=jax.ShapeDtypeStruct((depth, rp, n), F32),
        compiler_params=_params(("arbitrary", "arbitrary"), 40),
        name="ada_mod",
    )(c_all, w_ada, b_ada.reshape(depth, 1, n))


def _normmod_body(x_ref, g_ref, sc_ref, sh_ref, o_ref):
    x = x_ref[...]
    y = x * lax.rsqrt(jnp.mean(x * x, axis=-1, keepdims=True) + EPS) * g_ref[...]
    o_ref[...] = (y * (1 + sc_ref[...]) + sh_ref[...]).astype(BF16)


def _normmod_call(x, g, scale, shift, bb, lb):
    b, l, d = x.shape
    row = pl.BlockSpec((bb, lb, d), lambda i, r: (i, r, 0))
    per_b = pl.BlockSpec((bb, 1, d), lambda i, r: (i, 0, 0))
    return pl.pallas_call(
        _normmod_body,
        grid=(b // bb, l // lb),
        in_specs=[row, pl.BlockSpec((1, 1, d), lambda i, r: (0, 0, 0)), per_b, per_b],
        out_specs=row,
        out_shape=jax.ShapeDtypeStruct((b, l, d), BF16),
        compiler_params=_params(("arbitrary", "arbitrary"), 40),
        name="norm_mod",
    )(x, g.reshape(1, 1, d), scale, shift)


def _proj_body(a_ref, w_ref, *rest, mode, n_gelu_tiles):
    acc = _dot(a_ref[...], w_ref[...].astype(BF16))
    tn = acc.shape[1]
    if mode == "headnorm":
        g_ref, outs = rest[0], rest[1:]
        for hh in range(tn // HEAD_DIM):
            cs = slice(hh * HEAD_DIM, (hh + 1) * HEAD_DIM)
            blk = acc[:, cs]
            y = blk * lax.rsqrt(jnp.mean(blk * blk, axis=-1, keepdims=True) + EPS) * g_ref[...]
            for o_ref in outs:
                o_ref[:, cs] = y.astype(o_ref.dtype)
    elif mode == "plain":
        for o_ref in rest:
            o_ref[...] = acc.astype(o_ref.dtype)
    elif mode == "logsigmoid":
        b_ref, o_ref = rest
        x = acc + b_ref[...]
        o_ref[...] = jnp.minimum(x, 0.0) - jnp.log1p(jnp.exp(-jnp.abs(x)))
    elif mode == "silu":
        (o_ref,) = rest
        o_ref[...] = jax.nn.silu(acc).astype(o_ref.dtype)
    elif mode == "gelu_silu":
        (o_ref,) = rest
        j = pl.program_id(1)

        @pl.when(j < n_gelu_tiles)
        def _():
            o_ref[...] = (0.5 * acc * (1.0 + lax.erf(acc * (2.0 ** -0.5)))).astype(o_ref.dtype)

        @pl.when(j >= n_gelu_tiles)
        def _():
            o_ref[...] = jax.nn.silu(acc).astype(o_ref.dtype)
    else:
        raise ValueError(mode)


def _proj_call(a, w, mode, out_dtypes, extra=None, n_gelu_tiles=0, tm=1024, tn=512, layer=None, cols=None,
               name="proj"):
    m, k = a.shape
    col0, n = cols if cols is not None else (0, w.shape[-1])
    tn = min(tn, n)
    j0 = col0 // tn
    assert j0 * tn == col0 and n % tn == 0
    if layer is None:
        wspec = pl.BlockSpec((k, tn), lambda i, j: (0, j + j0))
    else:
        wspec = pl.BlockSpec((None, k, tn), lambda i, j: (layer, 0, j + j0))
    in_specs = [pl.BlockSpec((tm, k), lambda i, j: (i, 0)), wspec]
    args = [a, w]
    if extra is not None:
        ew = extra.shape[1]
        if ew == n:
            in_specs.append(pl.BlockSpec((1, tn), lambda i, j: (0, j)))
        else:
            in_specs.append(pl.BlockSpec((1, ew), lambda i, j: (0, 0)))
        args.append(extra)
    out_spec = pl.BlockSpec((tm, tn), lambda i, j: (i, j))
    outs = pl.pallas_call(
        functools.partial(_proj_body, mode=mode, n_gelu_tiles=n_gelu_tiles),
        grid=(m // tm, n // tn),
        in_specs=in_specs,
        out_specs=[out_spec] * len(out_dtypes),
        out_shape=[jax.ShapeDtypeStruct((m, n), dt) for dt in out_dtypes],
        compiler_params=_params(("arbitrary", "arbitrary"), 48),
        name=name,
    )(*args)
    return outs


def _outproj_body(*refs, n_a):
    a_refs, w_refs = refs[:n_a], refs[n_a:2 * n_a]
    x_ref, g_ref, o_ref = refs[2 * n_a:]
    bb, lb, tn = x_ref.shape
    acc = None
    for a_ref, w_ref in zip(a_refs, w_refs):
        a = a_ref[...].reshape(bb * lb, a_ref.shape[2])
        d = _dot(a, w_ref[...].astype(BF16))
        acc = d if acc is None else acc + d
    o_ref[...] = x_ref[...] + g_ref[...] * acc.reshape(bb, lb, tn)


def _outproj_call(a_list, w, layer, x, gate, bb, lb, tn=512):
    b, l, n = x.shape
    n_a = len(a_list)
    in_specs = []
    for a in a_list:
        in_specs.append(pl.BlockSpec((bb, lb, a.shape[2]), lambda i, r, j: (i, r, 0)))
    for ai, a in enumerate(a_list):
        in_specs.append(pl.BlockSpec((None, a.shape[2], tn), lambda i, r, j, ai=ai: (layer, ai, j)))
    xspec = pl.BlockSpec((bb, lb, tn), lambda i, r, j: (i, r, j))
    in_specs += [xspec, pl.BlockSpec((bb, 1, tn), lambda i, r, j: (i, 0, j))]
    return pl.pallas_call(
        functools.partial(_outproj_body, n_a=n_a),
        grid=(b // bb, l // lb, n // tn),
        in_specs=in_specs,
        out_specs=xspec,
        out_shape=jax.ShapeDtypeStruct((b, l, n), F32),
        compiler_params=_params(("arbitrary", "arbitrary", "arbitrary"), 48),
        name="out_proj",
    )(*a_list, *([w] * n_a), x, gate)


def _cumsum_body(x_ref, o_ref, *, ch, reverse_exclusive):
    rb, n = x_ref.shape
    ii = lax.broadcasted_iota(jnp.int32, (ch, ch), 0)
    jj = lax.broadcasted_iota(jnp.int32, (ch, ch), 1)
    tri = (ii > jj) if reverse_exclusive else (ii <= jj)
    tri = tri.astype(F32)
    chunks = range(n // ch)
    carry = jnp.zeros((rb, 1), F32)
    for c in (reversed(chunks) if reverse_exclusive else chunks):
        xc = x_ref[:, c * ch:(c + 1) * ch]
        y = jnp.dot(xc, tri, precision=lax.Precision.HIGHEST, preferred_element_type=F32)
        o_ref[:, c * ch:(c + 1) * ch] = y + carry
        carry = carry + jnp.sum(xc, axis=-1, keepdims=True)


def _cumsum_call(x, reverse_exclusive, rb):
    r, n = x.shape
    ch = min(n, 512)
    return pl.pallas_call(
        functools.partial(_cumsum_body, ch=ch, reverse_exclusive=reverse_exclusive),
        grid=(r // rb,),
        in_specs=[pl.BlockSpec((rb, n), lambda i: (i, 0))],
        out_specs=pl.BlockSpec((rb, n), lambda i: (i, 0)),
        out_shape=jax.ShapeDtypeStruct((r, n), F32),
        compiler_params=_params(("arbitrary",), 32),
        name="cumsum",
    )(x)


def _store_scores(raw, h, c1, qb_ref, kbias, mask, s_ref, pm_ref):
    tk = raw.shape[1]
    w = min(tk, LANE)
    pm = None
    for c in range(tk // w):
        cs = slice(c * w, (c + 1) * w)
        sc = raw[:, cs] * c1 + (qb_ref[h, :, :w] + kbias[:, cs])
        if mask is not None:
            sc = jnp.where(mask[:, cs], sc, NEG)
        s_ref[h, :, cs] = sc
        pm = sc if pm is None else jnp.maximum(pm, sc)
    pm_ref[h] = pm


def _softmax_update(s_ref, p_ref, pm_ref, m_ref, a_ref):
    nslots, _, tk = s_ref.shape
    w = pm_ref.shape[2]
    for h in range(nslots):
        m_prev = m_ref[h]
        m_new = jnp.maximum(m_prev, jnp.max(pm_ref[h], axis=-1, keepdims=True))
        a_ref[h] = jnp.exp2(m_prev - m_new)
        m_ref[h] = m_new
    for h in range(nslots):
        for c in range(tk // w):
            cs = slice(c * w, (c + 1) * w)
            p_ref[h, :, cs] = jnp.exp2(s_ref[h, :, cs] - m_ref[h, :, :w]).astype(BF16)


def _accumulate(h, p, v, a_ref, l_ref, acc_ref):
    res = _dot(p, jnp.concatenate([v, jnp.ones_like(v)], axis=1))
    a = a_ref[h]
    return a * acc_ref[h] + res[:, :HEAD_DIM], a * l_ref[h] + res[:, HEAD_DIM:]


def _causal_mask(rows, cols):
    rr = lax.broadcasted_iota(jnp.int32, (rows, cols), 0)
    cc = lax.broadcasted_iota(jnp.int32, (rows, cols), 1)
    return cc <= rr


def _init_softmax_state(m_ref, l_ref, acc_ref):
    m_ref[...] = jnp.full(m_ref.shape, NEG, F32)
    l_ref[...] = jnp.zeros(l_ref.shape, F32)
    acc_ref[...] = jnp.zeros(acc_ref.shape, F32)


def _head_cols(h):
    return slice(h * HEAD_DIM, (h + 1) * HEAD_DIM)


def _attn_prompt_body(q_ref, k_ref, v_ref, fcol_ref, frow_ref, sg_ref, o_ref,
                      s_ref, p_ref, pm_ref, m_ref, l_ref, a_ref, acc_ref, qb_ref, *, tq, hg, c1):
    g = pl.program_id(1)
    qi = pl.program_id(2)
    fblk = fcol_ref[0] * LOG2E
    lane = lax.broadcasted_iota(jnp.int32, fblk.shape, 1)
    for hh in range(hg):
        fc = jnp.sum(jnp.where(lane == g * hg + hh, fblk, 0.0), axis=-1, keepdims=True)
        qb_ref[hh] = jnp.broadcast_to(fc, qb_ref.shape[1:])
    _init_softmax_state(m_ref, l_ref, acc_ref)

    def step(kj, masked):
        rows = pl.ds(pl.multiple_of(kj * tq, tq), tq)
        mask = _causal_mask(tq, tq) if masked else None
        for hh in range(hg):
            raw = _dot_nt(q_ref[0, :, _head_cols(hh)], k_ref[0, rows, _head_cols(hh)])
            kbias = frow_ref[0, kj, hh:hh + 1, :] * -LOG2E
            _store_scores(raw, hh, c1, qb_ref, kbias, mask, s_ref, pm_ref)
        _softmax_update(s_ref, p_ref, pm_ref, m_ref, a_ref)
        for hh in range(hg):
            acc_ref[hh], l_ref[hh] = _accumulate(hh, p_ref[hh], v_ref[0, rows, _head_cols(hh)],
                                                 a_ref, l_ref, acc_ref)

    def loop_body(kj, carry):
        step(kj, False)
        return carry

    lax.fori_loop(0, qi, loop_body, 0)
    step(qi, True)
    for hh in range(hg):
        cs = _head_cols(hh)
        o_ref[0, :, cs] = (acc_ref[hh] / l_ref[hh] * sg_ref[0, :, cs]).astype(BF16)


def _attn_prompt_call(q, k, v, fcol, frow, sgate, tq=512, hg=4):
    b, s, aw = q.shape
    nh = aw // HEAD_DIM
    nq = s // tq
    gw = hg * HEAD_DIM
    qspec = pl.BlockSpec((1, tq, gw), lambda bi, g, qi: (bi, qi, g))
    kvspec = pl.BlockSpec((1, s, gw), lambda bi, g, qi: (bi, 0, g))
    frow4 = frow.reshape(b * nh // hg, hg, nq, tq).transpose(0, 2, 1, 3)
    rep = pltpu.VMEM((hg, tq, LANE), F32)
    return pl.pallas_call(
        functools.partial(_attn_prompt_body, tq=tq, hg=hg, c1=HEAD_DIM ** -0.5 * LOG2E),
        grid=(b, nh // hg, nq),
        in_specs=[qspec, kvspec, kvspec,
                  pl.BlockSpec((1, tq, nh), lambda bi, g, qi: (bi, qi, 0)),
                  pl.BlockSpec((1, nq, hg, tq), lambda bi, g, qi: (bi * (nh // hg) + g, 0, 0, 0)),
                  qspec],
        out_specs=qspec,
        out_shape=jax.ShapeDtypeStruct((b, s, aw), BF16),
        scratch_shapes=[pltpu.VMEM((hg, tq, tq), F32), pltpu.VMEM((hg, tq, tq), BF16), rep,
                        rep, rep, rep, rep, rep],
        compiler_params=_params(("arbitrary", "arbitrary", "arbitrary"), 48),
        name="attn_prompt",
    )(q, k, v, fcol, frow4, sgate)


def _cache_copy(hbm_ref, buf_ref, sem_ref, layer, bi, pi, slot, h):
    tp = buf_ref.shape[2]
    return pltpu.make_async_copy(hbm_ref.at[layer, bi, pl.ds(pi * tp, tp), h, :],
                                 buf_ref.at[slot, h], sem_ref.at[slot, h])


def _attn_sample_body(q_ref, kn_ref, vn_ref, ck_hbm, cv_hbm, fcol_ref, frow_ref, gp_ref, sg_ref, o_ref,
                      kbuf, vbuf, ksem, vsem,
                      s_ref, p_ref, pm_ref, sn_ref, pn_ref, pmn_ref, m_ref, l_ref, a_ref, acc_ref,
                      qb_ref, *, nh, c1, layer):
    bi = pl.program_id(0)
    pi = pl.program_id(1)
    n_p = pl.num_programs(1)
    l = q_ref.shape[1]
    step = bi * n_p + pi
    slot = lax.rem(step, 2)

    def fetch(b_to, p_to, slot_to):
        for h in range(nh):
            _cache_copy(ck_hbm, kbuf, ksem, layer, b_to, p_to, slot_to, h).start()
            _cache_copy(cv_hbm, vbuf, vsem, layer, b_to, p_to, slot_to, h).start()

    @pl.when(step == 0)
    def _():
        fetch(0, 0, 0)

    @pl.when(step + 1 < pl.num_programs(0) * n_p)
    def _():
        nxt = step + 1
        fetch(lax.div(nxt, n_p), lax.rem(nxt, n_p), 1 - slot)

    @pl.when(pi == 0)
    def _():
        _init_softmax_state(m_ref, l_ref, acc_ref)
        for h in range(nh):
            qb_ref[h] = jnp.broadcast_to(fcol_ref[0, :, h:h + 1] * LOG2E, qb_ref.shape[1:])

    for h in range(nh):
        _cache_copy(ck_hbm, kbuf, ksem, layer, bi, pi, slot, h).wait()
        _cache_copy(cv_hbm, vbuf, vsem, layer, bi, pi, slot, h).wait()
    for h in range(nh):
        raw = _dot_nt(q_ref[0, :, _head_cols(h)], kbuf[slot, h].astype(BF16))
        _store_scores(raw, h, c1, qb_ref, gp_ref[0, h:h + 1, :] * LOG2E, None, s_ref, pm_ref)
    _softmax_update(s_ref, p_ref, pm_ref, m_ref, a_ref)
    for h in range(nh):
        acc_ref[h], l_ref[h] = _accumulate(h, p_ref[h], vbuf[slot, h].astype(BF16), a_ref, l_ref, acc_ref)

    @pl.when(pi == pl.num_programs(1) - 1)
    def _():
        mask = _causal_mask(l, l)
        for h in range(nh):
            cs = _head_cols(h)
            raw = _dot_nt(q_ref[0, :, cs], kn_ref[0, :, cs])
            _store_scores(raw, h, c1, qb_ref, frow_ref[0, h:h + 1, :] * -LOG2E, mask, sn_ref, pmn_ref)
        _softmax_update(sn_ref, pn_ref, pmn_ref, m_ref, a_ref)
        for h in range(nh):
            cs = _head_cols(h)
            acc, den = _accumulate(h, pn_ref[h], vn_ref[0, :, cs], a_ref, l_ref, acc_ref)
            o_ref[0, :, cs] = (acc / den * sg_ref[0, :, cs]).astype(BF16)


def _attn_sample_call(q, kn, vn, ck, cv, layer, fcol, frow, gpast, sgate, tp=512):
    b, l, aw = q.shape
    nh = aw // HEAD_DIM
    p = ck.shape[2]
    new = pl.BlockSpec((1, l, aw), lambda bi, pi: (bi, 0, 0))
    cache = pl.BlockSpec(memory_space=pl.ANY)
    cbuf = pltpu.VMEM((2, nh, tp, HEAD_DIM), ck.dtype)
    csem = pltpu.SemaphoreType.DMA((2, nh))
    rep = pltpu.VMEM((nh, l, LANE), F32)
    lw = min(l, LANE)
    return pl.pallas_call(
        functools.partial(_attn_sample_body, nh=nh, c1=HEAD_DIM ** -0.5 * LOG2E, layer=layer),
        grid=(b, p // tp),
        in_specs=[new, new, new, cache, cache,
                  pl.BlockSpec((1, l, nh), lambda bi, pi: (bi, 0, 0)),
                  pl.BlockSpec((1, nh, l), lambda bi, pi: (bi, 0, 0)),
                  pl.BlockSpec((1, nh, tp), lambda bi, pi: (bi, 0, pi)),
                  new],
        out_specs=new,
        out_shape=jax.ShapeDtypeStruct((b, l, aw), BF16),
        scratch_shapes=[cbuf, cbuf, csem, csem,
                        pltpu.VMEM((nh, l, tp), F32), pltpu.VMEM((nh, l, tp), BF16), rep,
                        pltpu.VMEM((nh, l, l), F32), pltpu.VMEM((nh, l, l), BF16),
                        pltpu.VMEM((nh, l, lw), F32),
                        rep, rep, rep, rep, rep],
        compiler_params=_params(("arbitrary", "arbitrary"), 48),
        name="attn_sample",
    )(q, kn, vn, ck, cv, fcol, frow, gpast, sgate)


def _pool_body(u_ref, halo_ref, hist_ref, sg_ref, w_ref, ls_ref, o_ref, ext_ref, *, pos0):
    r = pl.program_id(1)
    tm = u_ref.shape[1]
    group = u_ref.shape[2] // len(POOL_WINDOWS)
    ext_ref[HALO:HALO + tm, :] = u_ref[0]

    @pl.when(r == 0)
    def _():
        ext_ref[0:HALO, :] = hist_ref[0]

    @pl.when(r > 0)
    def _():
        ext_ref[0:HALO, :] = halo_ref[0]

    n_before = lax.broadcasted_iota(jnp.int32, (tm, 1), 0) + (pos0 + 1) + r * tm
    for gi, w in enumerate(POOL_WINDOWS):
        cs = slice(gi * group, (gi + 1) * group)
        win = ext_ref[HALO:HALO + tm, cs]
        for i in range(1, w):
            win = win + ext_ref[HALO - i:HALO - i + tm, cs]
        cnt = jnp.minimum(w, n_before).astype(F32)
        d = win / cnt - u_ref[0, :, cs]
        y = _dot(d.astype(BF16), w_ref[gi]) * ls_ref[:, cs]
        o_ref[0, :, cs] = (y * sg_ref[0, :, cs]).astype(BF16)


def _pool_call(u, hist16, sgate, w_pool, ls_pool, pos0, tm):
    b, l, bw = u.shape
    g = w_pool.shape[0]
    row = pl.BlockSpec((1, tm, bw), lambda bi, r: (bi, r, 0))
    halo_blocks = tm // HALO
    return pl.pallas_call(
        functools.partial(_pool_body, pos0=pos0),
        grid=(b, l // tm),
        in_specs=[row,
                  pl.BlockSpec((1, HALO, bw), lambda bi, r: (bi, jnp.maximum(r * halo_blocks - 1, 0), 0)),
                  pl.BlockSpec((1, HALO, bw), lambda bi, r: (bi, 0, 0)),
                  pl.BlockSpec((1, tm, bw), lambda bi, r: (bi, r, 1)),
                  pl.BlockSpec((g, bw // g, bw // g), lambda bi, r: (0, 0, 0)),
                  pl.BlockSpec((1, bw), lambda bi, r: (0, 0))],
        out_specs=row,
        out_shape=jax.ShapeDtypeStruct((b, l, bw), BF16),
        scratch_shapes=[pltpu.VMEM((HALO + tm, bw), F32)],
        compiler_params=_params(("arbitrary", "arbitrary"), 40),
        name="pool_mix",
    )(u, u, hist16, sgate, w_pool, ls_pool.reshape(1, bw))


def _sgu_body(u_ref, v_ref, gt_ref, gv_ref, bv_ref, ws_ref, bst_ref, o_ref, *vout, cl):
    lb, cw = u_ref.shape[1], u_ref.shape[2]
    gw = cw // N_SGU_GROUPS
    rr = lax.broadcasted_iota(jnp.int32, (cl, cl), 0)
    cc = lax.broadcasted_iota(jnp.int32, (cl, cl), 1)
    for c in range(lb // cl):
        rows = slice(c * cl, (c + 1) * cl)
        v = v_ref[0, rows, :].astype(F32)
        xc = v - jnp.mean(v, axis=-1, keepdims=True)
        var = jnp.mean(xc * xc, axis=-1, keepdims=True)
        vln = xc * lax.rsqrt(var + EPS) * gv_ref[...] + bv_ref[...]
        if vout:
            vout[0][0, rows, :] = vln
        for g in range(N_SGU_GROUPS):
            cs = slice(g * gw, (g + 1) * gw)
            ws = jnp.where(cc <= rr, ws_ref[g, :cl, :cl], 0.0).astype(BF16)
            sv = _dot(ws, vln[:, cs].astype(BF16)) + bst_ref[:cl, g:g + 1]
            o_ref[0, rows, cs] = (u_ref[0, rows, cs] * sv * gt_ref[0, rows, cs]).astype(BF16)


def _sgu_call(zact, g_v, b_v, w_s, b_s_t, lb, cl, want_v):
    b, l, cw3 = zact.shape
    cw = cw3 // 3
    out_spec = pl.BlockSpec((1, lb, cw), lambda bi, r: (bi, r, 0))
    out_shape = [jax.ShapeDtypeStruct((b, l, cw), BF16)]
    out_specs = [out_spec]
    if want_v:
        out_shape.append(jax.ShapeDtypeStruct((b, l, cw), F32))
        out_specs.append(out_spec)
    vec = pl.BlockSpec((1, cw), lambda bi, r: (0, 0))
    return pl.pallas_call(
        functools.partial(_sgu_body, cl=cl),
        grid=(b, l // lb),
        in_specs=[pl.BlockSpec((1, lb, cw), lambda bi, r: (bi, r, 0)),
                  pl.BlockSpec((1, lb, cw), lambda bi, r: (bi, r, 1)),
                  pl.BlockSpec((1, lb, cw), lambda bi, r: (bi, r, 2)),
                  vec, vec,
                  pl.BlockSpec(w_s.shape, lambda bi, r: (0, 0, 0)),
                  pl.BlockSpec(b_s_t.shape, lambda bi, r: (0, 0))],
        out_specs=out_specs,
        out_shape=out_shape,
        compiler_params=_params(("arbitrary", "arbitrary"), 48),
        name="sgu",
    )(zact, zact, zact, g_v.reshape(1, cw), b_v.reshape(1, cw), w_s, b_s_t)


def _row_blocking(b, l, rows=1024):
    if l >= rows:
        return 1, rows
    return rows // l, l


def _layer_ab(x, shift, scale, gate, g_norm, wts, hist, pos0, cache):
    b, l, d = x.shape
    aw = wts["cols"]["q"][1]
    nh = aw // HEAD_DIM
    bb, lb = _row_blocking(b, l)
    h = _normmod_call(x, g_norm, scale, shift, *_row_blocking(b, l, 256)).reshape(b * l, d)

    w_in, col = wts["win"], wts["cols"]
    (qn,) = _proj_call(h, w_in, "headnorm", [BF16], extra=wts["gq"], cols=col["q"], name="proj_q")
    k32, k16 = _proj_call(h, w_in, "headnorm", [F32, BF16], extra=wts["gk"], cols=col["k"], name="proj_k")
    v32, v16 = _proj_call(h, w_in, "plain", [F32, BF16], cols=col["v"], name="proj_v")
    (logf_pad,) = _proj_call(h, w_in, "logsigmoid", [F32], extra=wts["bf"], cols=col["f"], name="proj_f")
    (u,) = _proj_call(h, w_in, "plain", [F32], cols=col["u"], name="proj_u")
    (sgate,) = _proj_call(h, w_in, "silu", [BF16], cols=col["g"], name="proj_gate")

    logf = logf_pad[:, :nh].reshape(b, l, nh)
    logf_t = logf.transpose(0, 2, 1).reshape(b * nh, l)
    fcs = _cumsum_call(logf_t, False, min(b * nh, 64))
    frow = fcs.reshape(b, nh, l)
    fcol = frow.transpose(0, 2, 1)

    q3, k3, v3 = (t.reshape(b, l, aw) for t in (qn, k16, v16))
    sgate3 = sgate.reshape(b, l, -1)
    if cache is None:
        mixed_a = _attn_prompt_call(q3, k3, v3, fcol, frow, sgate3)
    else:
        ck, cv, clogf, layer = cache
        p = ck.shape[2]
        clf_t = clogf.transpose(0, 2, 1).reshape(b * nh, p)
        gpast = _cumsum_call(clf_t, True, 64).reshape(b, nh, p)
        mixed_a = _attn_sample_call(q3, k3, v3, ck, cv, layer, fcol, frow, gpast, sgate3)

    u3 = u.reshape(b, l, -1)
    hist16 = jnp.pad(hist, ((0, 0), (HALO - POOL_HIST, 0), (0, 0)))
    mixed_b = _pool_call(u3, hist16, sgate3, wts["wpool"], wts["lspool"], pos0, min(l, 256))

    y = _outproj_call([mixed_a, mixed_b], wts["wo"], wts["idx"], x, gate, bb, lb)
    if l >= POOL_HIST:
        new_hist = u3[:, l - POOL_HIST:]
    else:
        new_hist = jnp.concatenate([hist, u3], axis=1)[:, -POOL_HIST:]
    return (y, k32.reshape(b, l, nh, HEAD_DIM), v32.reshape(b, l, nh, HEAD_DIM), logf, new_hist)


def _layer_c(x, shift, scale, gate, g_norm, wts, want_v):
    b, l, d = x.shape
    bb, lb = _row_blocking(b, l)
    h = _normmod_call(x, g_norm, scale, shift, *_row_blocking(b, l, 256)).reshape(b * l, d)
    cw = wts["wo"].shape[1]
    tn = 512
    (zact,) = _proj_call(h, wts["win"], "gelu_silu", [BF16], n_gelu_tiles=2 * cw // tn, tn=tn,
                         layer=wts["idx"], name="proj_c")
    cl = min(l, SGU_CHUNK)
    outs = _sgu_call(zact.reshape(b, l, 3 * cw), wts["gv"], wts["bv"], wts["ws"], wts["bst"],
                     min(l, 2 * SGU_CHUNK), cl, want_v)
    y = _outproj_call([outs[0]], wts["wo"], wts["idx"], x, gate, bb, lb)
    return y, (outs[1] if want_v else None)


def kernel(x_prompt, x_sample, cache_k, cache_v, cache_logf, state_pool, c_prompt, c_sample,
           w_ada, b_ada, g_norm, w_in_ab, b_forget, g_q, g_k, w_pool, ls_pool, w_out_ab,
           w_in_c, g_v, b_v, w_s, b_s, w_out_c):
    bp, sp, d = x_prompt.shape
    bs = x_sample.shape[0]
    depth = w_ada.shape[0]
    nh = cache_k.shape[3]
    aw = nh * HEAD_DIM
    bw = w_pool.shape[2] * w_pool.shape[1]
    past_len = cache_k.shape[2]

    c_all = jnp.concatenate([c_prompt, c_sample], axis=0)
    c_all = jnp.pad(c_all, ((0, -c_all.shape[0] % 8), (0, 0)))
    mod = _ada_call(c_all, w_ada, b_ada)

    def mods(layer, lo, n):
        m = mod[layer, lo:lo + n].reshape(n, 1, 3 * d)
        return m[..., :d], m[..., d:2 * d], m[..., 2 * d:]

    yp, ys = x_prompt, x_sample
    outs_p, outs_s, sgu_v = [], [], []
    for layer in range(depth):
        i = layer // 2
        shp, scp, gp = mods(layer, 0, bp)
        shs, scs, gs = mods(layer, bp, bs)
        if layer % 2 == 0:
            w0 = w_in_ab[i]
            o_f, o_u, o_g = 3 * aw, 3 * aw + nh, 3 * aw + nh + bw
            gw = w0.shape[1] - o_g
            w_cat = jnp.concatenate(
                [w0[:, :o_f], w0[:, o_u:], jnp.pad(w0[:, o_f:o_u], ((0, 0), (0, LANE - nh)))], axis=1).astype(BF16)
            wts = {
                "win": w_cat,
                "cols": {"q": (0, aw), "k": (aw, aw), "v": (2 * aw, aw), "u": (3 * aw, bw),
                         "g": (3 * aw + bw, gw), "f": (3 * aw + bw + gw, LANE)},
                "bf": jnp.pad(b_forget[i], (0, LANE - nh)).reshape(1, LANE),
                "gq": g_q[i].reshape(1, HEAD_DIM),
                "gk": g_k[i].reshape(1, HEAD_DIM),
                "wpool": w_pool[i].astype(BF16),
                "lspool": ls_pool[i],
                "wo": w_out_ab, "idx": i,
            }
            zero_hist = jnp.zeros((bp, POOL_HIST, bw), F32)
            rp = _layer_ab(yp, shp, scp, gp, g_norm[layer], wts, zero_hist, 0, None)
            rs = _layer_ab(ys, shs, scs, gs, g_norm[layer], wts, state_pool[i], past_len,
                           (cache_k, cache_v, cache_logf[i], i))
            yp, ys = rp[0], rs[0]
            outs_p.append(rp[1:])
            outs_s.append(rs[1:])
        else:
            wts = {
                "win": w_in_c, "wo": w_out_c, "idx": i,
                "gv": g_v[i], "bv": b_v[i],
                "ws": w_s[i], "bst": b_s[i].T,
            }
            yp, _ = _layer_c(yp, shp, scp, gp, g_norm[layer], wts, False)
            ys, v_c = _layer_c(ys, shs, scs, gs, g_norm[layer], wts, True)
            sgu_v.append(v_c)

    def stack(group, idx):
        return jnp.stack([o[idx] for o in group])

    return (yp, ys,
            stack(outs_p, 0), stack(outs_p, 1), stack(outs_p, 2), stack(outs_p, 3),
            stack(outs_s, 0), stack(outs_s, 1), stack(outs_s, 2), stack(outs_s, 3),
            jnp.stack(sgu_v))
```

```python
import functools

import jax
import jax.numpy as jnp
from jax import lax
from jax.experimental import pallas as pl
from jax.experimental.pallas import tpu as pltpu

F32 = jnp.float32
BF16 = jnp.bfloat16

EPS = 1e-6
HEAD_DIM = 128
POOL_WINDOWS = (2, 4, 8, 16)
POOL_HIST = max(POOL_WINDOWS) - 1
HALO = POOL_HIST + 1
SGU_CHUNK = 128
N_SGU_GROUPS = 16
NEG = -1e30
LOG2E = 1.4426950408889634
LANE = 128
MIB = 1024 * 1024


def _params(sem, vmem_mib):
    return pltpu.CompilerParams(dimension_semantics=sem, vmem_limit_bytes=vmem_mib * MIB)


def _dot(a, b):
    return jnp.dot(a, b, preferred_element_type=F32)


def _dot_nt(a, b):
    return lax.dot_general(a, b, (((1,), (1,)), ((), ())), preferred_element_type=F32)


def _ada_body(c_ref, w_ref, b_ref, o_ref):
    a = jax.nn.silu(c_ref[...]).astype(BF16)
    o_ref[0] = _dot(a, w_ref[0].astype(BF16)) + b_ref[0]


def _ada_call(c_all, w_ada, b_ada, tn=512):
    depth, d, n = w_ada.shape
    rp = c_all.shape[0]
    return pl.pallas_call(
        _ada_body,
        grid=(depth, n // tn),
        in_specs=[pl.BlockSpec((rp, d), lambda l, j: (0, 0)),
                  pl.BlockSpec((1, d, tn), lambda l, j: (l, 0, j)),
                  pl.BlockSpec((1, 1, tn), lambda l, j: (l, 0, j))],
        out_specs=pl.BlockSpec((1, rp, tn), lambda l, j: (l, 0, j)),
        out_shape=jax.ShapeDtypeStruct((depth, rp, n), F32),
        compiler_params=_params(("arbitrary", "arbitrary"), 40),
        name="ada_mod",
    )(c_all, w_ada, b_ada.reshape(depth, 1, n))


def _normmod_body(x_ref, g_ref, sc_ref, sh_ref, o_ref):
    x = x_ref[...]
    y = x * lax.rsqrt(jnp.mean(x * x, axis=-1, keepdims=True) + EPS) * g_ref[...]
    o_ref[...] = (y * (1 + sc_ref[...]) + sh_ref[...]).astype(BF16)


def _normmod_call(x, g, scale, shift, bb, lb):
    b, l, d = x.shape
    row = pl.BlockSpec((bb, lb, d), lambda i, r: (i, r, 0))
    per_b = pl.BlockSpec((bb, 1, d), lambda i, r: (i, 0, 0))
    return pl.pallas_call(
        _normmod_body,
        grid=(b // bb, l // lb),
        in_specs=[row, pl.BlockSpec((1, 1, d), lambda i, r: (0, 0, 0)), per_b, per_b],
        out_specs=row,
        out_shape=jax.ShapeDtypeStruct((b, l, d), BF16),
        compiler_params=_params(("arbitrary", "arbitrary"), 40),
        name="norm_mod",
    )(x, g.reshape(1, 1, d), scale, shift)


def _proj_body(a_ref, w_ref, *rest, mode, n_gelu_tiles, w_transposed):
    w = w_ref[...].astype(BF16)
    acc = _dot_nt(a_ref[...], w) if w_transposed else _dot(a_ref[...], w)
    tn = acc.shape[1]
    if mode == "headnorm":
        g_ref, outs = rest[0], rest[1:]
        for hh in range(tn // HEAD_DIM):
            cs = slice(hh * HEAD_DIM, (hh + 1) * HEAD_DIM)
            blk = acc[:, cs]
            y = blk * lax.rsqrt(jnp.mean(blk * blk, axis=-1, keepdims=True) + EPS) * g_ref[...]
            for o_ref in outs:
                o_ref[:, cs] = y.astype(o_ref.dtype)
    elif mode == "plain":
        for o_ref in rest:
            o_ref[...] = acc.astype(o_ref.dtype)
    elif mode == "logsigmoid":
        b_ref, o_ref = rest
        x = acc + b_ref[...]
        o_ref[...] = jnp.minimum(x, 0.0) - jnp.log1p(jnp.exp(-jnp.abs(x)))
    elif mode == "silu":
        (o_ref,) = rest
        o_ref[...] = jax.nn.silu(acc).astype(o_ref.dtype)
    elif mode == "gelu_silu":
        (o_ref,) = rest
        j = pl.program_id(1)

        @pl.when(j < n_gelu_tiles)
        def _():
            o_ref[...] = (0.5 * acc * (1.0 + lax.erf(acc * (2.0 ** -0.5)))).astype(o_ref.dtype)

        @pl.when(j >= n_gelu_tiles)
        def _():
            o_ref[...] = jax.nn.silu(acc).astype(o_ref.dtype)
    else:
        raise ValueError(mode)


def _proj_call(a, w, mode, out_dtypes, extra=None, n_gelu_tiles=0, tm=1024, tn=512, layer=0, cols=None,
               w_transposed=False, name="proj"):
    m, k = a.shape
    col0, n = cols if cols is not None else (0, w.shape[1 if w_transposed else 2])
    tn = min(tn, n)
    assert n % tn == 0
    if w_transposed:
        sub = 8
        assert col0 % sub == 0 and tn % sub == 0
        wspec = pl.BlockSpec((None, pl.Element(tn), pl.Element(k)),
                             lambda i, j: (layer, (col0 // sub + j * (tn // sub)) * sub, 0))
    else:
        j0 = col0 // tn
        assert j0 * tn == col0
        wspec = pl.BlockSpec((None, k, tn), lambda i, j: (layer, 0, j + j0))
    in_specs = [pl.BlockSpec((tm, k), lambda i, j: (i, 0)), wspec]
    args = [a, w]
    if extra is not None:
        ew = extra.shape[1]
        if ew == n:
            in_specs.append(pl.BlockSpec((1, tn), lambda i, j: (0, j)))
        else:
            in_specs.append(pl.BlockSpec((1, ew), lambda i, j: (0, 0)))
        args.append(extra)
    out_spec = pl.BlockSpec((tm, tn), lambda i, j: (i, j))
    outs = pl.pallas_call(
        functools.partial(_proj_body, mode=mode, n_gelu_tiles=n_gelu_tiles, w_transposed=w_transposed),
        grid=(m // tm, n // tn),
        in_specs=in_specs,
        out_specs=[out_spec] * len(out_dtypes),
        out_shape=[jax.ShapeDtypeStruct((m, n), dt) for dt in out_dtypes],
        compiler_params=_params(("arbitrary", "arbitrary"), 48),
        name=name,
    )(*args)
    return outs


def _outproj_body(*refs, n_a):
    a_refs, w_refs = refs[:n_a], refs[n_a:2 * n_a]
    x_ref, g_ref, o_ref = refs[2 * n_a:]
    bb, lb, tn = x_ref.shape
    acc = None
    for a_ref, w_ref in zip(a_refs, w_refs):
        a = a_ref[...].reshape(bb * lb, a_ref.shape[2])
        d = _dot(a, w_ref[...].astype(BF16))
        acc = d if acc is None else acc + d
    o_ref[...] = x_ref[...] + g_ref[...] * acc.reshape(bb, lb, tn)


def _outproj_call(a_list, w, layer, x, gate, bb, lb, tn=512):
    b, l, n = x.shape
    n_a = len(a_list)
    in_specs = []
    for a in a_list:
        in_specs.append(pl.BlockSpec((bb, lb, a.shape[2]), lambda i, r, j: (i, r, 0)))
    for ai, a in enumerate(a_list):
        in_specs.append(pl.BlockSpec((None, a.shape[2], tn), lambda i, r, j, ai=ai: (layer, ai, j)))
    xspec = pl.BlockSpec((bb, lb, tn), lambda i, r, j: (i, r, j))
    in_specs += [xspec, pl.BlockSpec((bb, 1, tn), lambda i, r, j: (i, 0, j))]
    return pl.pallas_call(
        functools.partial(_outproj_body, n_a=n_a),
        grid=(b // bb, l // lb, n // tn),
        in_specs=in_specs,
        out_specs=xspec,
        out_shape=jax.ShapeDtypeStruct((b, l, n), F32),
        compiler_params=_params(("arbitrary", "arbitrary", "arbitrary"), 48),
        name="out_proj",
    )(*a_list, *([w] * n_a), x, gate)


def _cumsum_body(x_ref, o_ref, *, ch, reverse_exclusive):
    rb, n = x_ref.shape
    ii = lax.broadcasted_iota(jnp.int32, (ch, ch), 0)
    jj = lax.broadcasted_iota(jnp.int32, (ch, ch), 1)
    tri = (ii > jj) if reverse_exclusive else (ii <= jj)
    tri = tri.astype(F32)
    chunks = range(n // ch)
    carry = jnp.zeros((rb, 1), F32)
    for c in (reversed(chunks) if reverse_exclusive else chunks):
        xc = x_ref[:, c * ch:(c + 1) * ch]
        y = jnp.dot(xc, tri, precision=lax.Precision.HIGHEST, preferred_element_type=F32)
        o_ref[:, c * ch:(c + 1) * ch] = y + carry
        carry = carry + jnp.sum(xc, axis=-1, keepdims=True)


def _cumsum_call(x, reverse_exclusive, rb):
    r, n = x.shape
    ch = min(n, 512)
    return pl.pallas_call(
        functools.partial(_cumsum_body, ch=ch, reverse_exclusive=reverse_exclusive),
        grid=(r // rb,),
        in_specs=[pl.BlockSpec((rb, n), lambda i: (i, 0))],
        out_specs=pl.BlockSpec((rb, n), lambda i: (i, 0)),
        out_shape=jax.ShapeDtypeStruct((r, n), F32),
        compiler_params=_params(("arbitrary",), 32),
        name="cumsum",
    )(x)


def _store_scores(raw, h, c1, qb_ref, kbias, mask, s_ref, pm_ref):
    tk = raw.shape[1]
    w = min(tk, LANE)
    pm = None
    for c in range(tk // w):
        cs = slice(c * w, (c + 1) * w)
        sc = raw[:, cs] * c1 + (qb_ref[h, :, :w] + kbias[:, cs])
        if mask is not None:
            sc = jnp.where(mask[:, cs], sc, NEG)
        s_ref[h, :, cs] = sc
        pm = sc if pm is None else jnp.maximum(pm, sc)
    pm_ref[h] = pm


def _softmax_update(s_ref, p_ref, pm_ref, m_ref, a_ref):
    nslots, _, tk = s_ref.shape
    w = pm_ref.shape[2]
    for h in range(nslots):
        m_prev = m_ref[h]
        m_new = jnp.maximum(m_prev, jnp.max(pm_ref[h], axis=-1, keepdims=True))
        a_ref[h] = jnp.exp2(m_prev - m_new)
        m_ref[h] = m_new
    for h in range(nslots):
        for c in range(tk // w):
            cs = slice(c * w, (c + 1) * w)
            p_ref[h, :, cs] = jnp.exp2(s_ref[h, :, cs] - m_ref[h, :, :w]).astype(BF16)


def _accumulate(h, p, v, a_ref, l_ref, acc_ref):
    res = _dot(p, jnp.concatenate([v, jnp.ones_like(v)], axis=1))
    a = a_ref[h]
    return a * acc_ref[h] + res[:, :HEAD_DIM], a * l_ref[h] + res[:, HEAD_DIM:]


def _causal_mask(rows, cols):
    rr = lax.broadcasted_iota(jnp.int32, (rows, cols), 0)
    cc = lax.broadcasted_iota(jnp.int32, (rows, cols), 1)
    return cc <= rr


def _init_softmax_state(m_ref, l_ref, acc_ref):
    m_ref[...] = jnp.full(m_ref.shape, NEG, F32)
    l_ref[...] = jnp.zeros(l_ref.shape, F32)
    acc_ref[...] = jnp.zeros(acc_ref.shape, F32)


def _head_cols(h):
    return slice(h * HEAD_DIM, (h + 1) * HEAD_DIM)


def _attn_prompt_body(q_ref, k_ref, v_ref, fcol_ref, frow_ref, sg_ref, o_ref,
                      s_ref, p_ref, pm_ref, m_ref, l_ref, a_ref, acc_ref, qb_ref, *, tq, hg, c1):
    g = pl.program_id(1)
    qi = pl.program_id(2)
    fblk = fcol_ref[0] * LOG2E
    lane = lax.broadcasted_iota(jnp.int32, fblk.shape, 1)
    for hh in range(hg):
        fc = jnp.sum(jnp.where(lane == g * hg + hh, fblk, 0.0), axis=-1, keepdims=True)
        qb_ref[hh] = jnp.broadcast_to(fc, qb_ref.shape[1:])
    _init_softmax_state(m_ref, l_ref, acc_ref)

    def step(kj, masked):
        rows = pl.ds(pl.multiple_of(kj * tq, tq), tq)
        mask = _causal_mask(tq, tq) if masked else None
        for hh in range(hg):
            raw = _dot_nt(q_ref[0, :, _head_cols(hh)], k_ref[0, rows, _head_cols(hh)])
            kbias = frow_ref[0, kj, hh:hh + 1, :] * -LOG2E
            _store_scores(raw, hh, c1, qb_ref, kbias, mask, s_ref, pm_ref)
        _softmax_update(s_ref, p_ref, pm_ref, m_ref, a_ref)
        for hh in range(hg):
            acc_ref[hh], l_ref[hh] = _accumulate(hh, p_ref[hh], v_ref[0, rows, _head_cols(hh)],
                                                 a_ref, l_ref, acc_ref)

    def loop_body(kj, carry):
        step(kj, False)
        return carry

    lax.fori_loop(0, qi, loop_body, 0)
    step(qi, True)
    for hh in range(hg):
        cs = _head_cols(hh)
        o_ref[0, :, cs] = (acc_ref[hh] / l_ref[hh] * sg_ref[0, :, cs]).astype(BF16)


def _attn_prompt_call(q, k, v, fcol, frow, sgate, tq=512, hg=4):
    b, s, aw = q.shape
    nh = aw // HEAD_DIM
    nq = s // tq
    gw = hg * HEAD_DIM
    qspec = pl.BlockSpec((1, tq, gw), lambda bi, g, qi: (bi, qi, g))
    kvspec = pl.BlockSpec((1, s, gw), lambda bi, g, qi: (bi, 0, g))
    frow4 = frow.reshape(b * nh // hg, hg, nq, tq).transpose(0, 2, 1, 3)
    rep = pltpu.VMEM((hg, tq, LANE), F32)
    return pl.pallas_call(
        functools.partial(_attn_prompt_body, tq=tq, hg=hg, c1=HEAD_DIM ** -0.5 * LOG2E),
        grid=(b, nh // hg, nq),
        in_specs=[qspec, kvspec, kvspec,
                  pl.BlockSpec((1, tq, nh), lambda bi, g, qi: (bi, qi, 0)),
                  pl.BlockSpec((1, nq, hg, tq), lambda bi, g, qi: (bi * (nh // hg) + g, 0, 0, 0)),
                  qspec],
        out_specs=qspec,
        out_shape=jax.ShapeDtypeStruct((b, s, aw), BF16),
        scratch_shapes=[pltpu.VMEM((hg, tq, tq), F32), pltpu.VMEM((hg, tq, tq), BF16), rep,
                        rep, rep, rep, rep, rep],
        compiler_params=_params(("arbitrary", "arbitrary", "arbitrary"), 48),
        name="attn_prompt",
    )(q, k, v, fcol, frow4, sgate)


def _cache_copy(hbm_ref, buf_ref, sem_ref, layer, bi, pi, slot, h):
    tp = buf_ref.shape[2]
    return pltpu.make_async_copy(hbm_ref.at[layer, bi, pl.ds(pi * tp, tp), h, :],
                                 buf_ref.at[slot, h], sem_ref.at[slot, h])


def _attn_sample_body(q_ref, kn_ref, vn_ref, ck_hbm, cv_hbm, fcol_ref, frow_ref, gp_ref, sg_ref, o_ref,
                      kbuf, vbuf, ksem, vsem,
                      s_ref, p_ref, pm_ref, sn_ref, pn_ref, pmn_ref, m_ref, l_ref, a_ref, acc_ref,
                      qb_ref, *, nh, c1, layer):
    bi = pl.program_id(0)
    pi = pl.program_id(1)
    n_p = pl.num_programs(1)
    l = q_ref.shape[1]
    step = bi * n_p + pi
    slot = lax.rem(step, 2)

    def fetch(b_to, p_to, slot_to):
        for h in range(nh):
            _cache_copy(ck_hbm, kbuf, ksem, layer, b_to, p_to, slot_to, h).start()
            _cache_copy(cv_hbm, vbuf, vsem, layer, b_to, p_to, slot_to, h).start()

    @pl.when(step == 0)
    def _():
        fetch(0, 0, 0)

    @pl.when(step + 1 < pl.num_programs(0) * n_p)
    def _():
        nxt = step + 1
        fetch(lax.div(nxt, n_p), lax.rem(nxt, n_p), 1 - slot)

    @pl.when(pi == 0)
    def _():
        _init_softmax_state(m_ref, l_ref, acc_ref)
        for h in range(nh):
            qb_ref[h] = jnp.broadcast_to(fcol_ref[0, :, h:h + 1] * LOG2E, qb_ref.shape[1:])

    for h in range(nh):
        _cache_copy(ck_hbm, kbuf, ksem, layer, bi, pi, slot, h).wait()
        _cache_copy(cv_hbm, vbuf, vsem, layer, bi, pi, slot, h).wait()
    for h in range(nh):
        raw = _dot_nt(q_ref[0, :, _head_cols(h)], kbuf[slot, h].astype(BF16))
        _store_scores(raw, h, c1, qb_ref, gp_ref[0, h:h + 1, :] * LOG2E, None, s_ref, pm_ref)
    _softmax_update(s_ref, p_ref, pm_ref, m_ref, a_ref)
    for h in range(nh):
        acc_ref[h], l_ref[h] = _accumulate(h, p_ref[h], vbuf[slot, h].astype(BF16), a_ref, l_ref, acc_ref)

    @pl.when(pi == pl.num_programs(1) - 1)
    def _():
        mask = _causal_mask(l, l)
        for h in range(nh):
            cs = _head_cols(h)
            raw = _dot_nt(q_ref[0, :, cs], kn_ref[0, :, cs])
            _store_scores(raw, h, c1, qb_ref, frow_ref[0, h:h + 1, :] * -LOG2E, mask, sn_ref, pmn_ref)
        _softmax_update(sn_ref, pn_ref, pmn_ref, m_ref, a_ref)
        for h in range(nh):
            cs = _head_cols(h)
            acc, den = _accumulate(h, pn_ref[h], vn_ref[0, :, cs], a_ref, l_ref, acc_ref)
            o_ref[0, :, cs] = (acc / den * sg_ref[0, :, cs]).astype(BF16)


def _attn_sample_call(q, kn, vn, ck, cv, layer, fcol, frow, gpast, sgate, tp=512):
    b, l, aw = q.shape
    nh = aw // HEAD_DIM
    p = ck.shape[2]
    new = pl.BlockSpec((1, l, aw), lambda bi, pi: (bi, 0, 0))
    cache = pl.BlockSpec(memory_space=pl.ANY)
    cbuf = pltpu.VMEM((2, nh, tp, HEAD_DIM), ck.dtype)
    csem = pltpu.SemaphoreType.DMA((2, nh))
    rep = pltpu.VMEM((nh, l, LANE), F32)
    lw = min(l, LANE)
    return pl.pallas_call(
        functools.partial(_attn_sample_body, nh=nh, c1=HEAD_DIM ** -0.5 * LOG2E, layer=layer),
        grid=(b, p // tp),
        in_specs=[new, new, new, cache, cache,
                  pl.BlockSpec((1, l, nh), lambda bi, pi: (bi, 0, 0)),
                  pl.BlockSpec((1, nh, l), lambda bi, pi: (bi, 0, 0)),
                  pl.BlockSpec((1, nh, tp), lambda bi, pi: (bi, 0, pi)),
                  new],
        out_specs=new,
        out_shape=jax.ShapeDtypeStruct((b, l, aw), BF16),
        scratch_shapes=[cbuf, cbuf, csem, csem,
                        pltpu.VMEM((nh, l, tp), F32), pltpu.VMEM((nh, l, tp), BF16), rep,
                        pltpu.VMEM((nh, l, l), F32), pltpu.VMEM((nh, l, l), BF16),
                        pltpu.VMEM((nh, l, lw), F32),
                        rep, rep, rep, rep, rep],
        compiler_params=_params(("arbitrary", "arbitrary"), 48),
        name="attn_sample",
    )(q, kn, vn, ck, cv, fcol, frow, gpast, sgate)


def _pool_body(u_ref, halo_ref, hist_ref, sg_ref, w_ref, ls_ref, o_ref, ext_ref, *, pos0):
    r = pl.program_id(1)
    tm = u_ref.shape[1]
    group = u_ref.shape[2] // len(POOL_WINDOWS)
    ext_ref[HALO:HALO + tm, :] = u_ref[0]

    @pl.when(r == 0)
    def _():
        ext_ref[0:HALO, :] = hist_ref[0]

    @pl.when(r > 0)
    def _():
        ext_ref[0:HALO, :] = halo_ref[0]

    n_before = lax.broadcasted_iota(jnp.int32, (tm, 1), 0) + (pos0 + 1) + r * tm
    for gi, w in enumerate(POOL_WINDOWS):
        cs = slice(gi * group, (gi + 1) * group)
        win = ext_ref[HALO:HALO + tm, cs]
        for i in range(1, w):
            win = win + ext_ref[HALO - i:HALO - i + tm, cs]
        cnt = jnp.minimum(w, n_before).astype(F32)
        d = win / cnt - u_ref[0, :, cs]
        y = _dot(d.astype(BF16), w_ref[gi]) * ls_ref[:, cs]
        o_ref[0, :, cs] = (y * sg_ref[0, :, cs]).astype(BF16)


def _pool_call(u, hist16, sgate, w_pool, ls_pool, pos0, tm):
    b, l, bw = u.shape
    g = w_pool.shape[0]
    row = pl.BlockSpec((1, tm, bw), lambda bi, r: (bi, r, 0))
    halo_blocks = tm // HALO
    return pl.pallas_call(
        functools.partial(_pool_body, pos0=pos0),
        grid=(b, l // tm),
        in_specs=[row,
                  pl.BlockSpec((1, HALO, bw), lambda bi, r: (bi, jnp.maximum(r * halo_blocks - 1, 0), 0)),
                  pl.BlockSpec((1, HALO, bw), lambda bi, r: (bi, 0, 0)),
                  pl.BlockSpec((1, tm, bw), lambda bi, r: (bi, r, 1)),
                  pl.BlockSpec((g, bw // g, bw // g), lambda bi, r: (0, 0, 0)),
                  pl.BlockSpec((1, bw), lambda bi, r: (0, 0))],
        out_specs=row,
        out_shape=jax.ShapeDtypeStruct((b, l, bw), BF16),
        scratch_shapes=[pltpu.VMEM((HALO + tm, bw), F32)],
        compiler_params=_params(("arbitrary", "arbitrary"), 40),
        name="pool_mix",
    )(u, u, hist16, sgate, w_pool, ls_pool.reshape(1, bw))


def _sgu_body(u_ref, v_ref, gt_ref, gv_ref, bv_ref, ws_ref, bst_ref, o_ref, *vout, cl):
    lb, cw = u_ref.shape[1], u_ref.shape[2]
    gw = cw // N_SGU_GROUPS
    rr = lax.broadcasted_iota(jnp.int32, (cl, cl), 0)
    cc = lax.broadcasted_iota(jnp.int32, (cl, cl), 1)
    for c in range(lb // cl):
        rows = slice(c * cl, (c + 1) * cl)
        v = v_ref[0, rows, :].astype(F32)
        xc = v - jnp.mean(v, axis=-1, keepdims=True)
        var = jnp.mean(xc * xc, axis=-1, keepdims=True)
        vln = xc * lax.rsqrt(var + EPS) * gv_ref[...] + bv_ref[...]
        if vout:
            vout[0][0, rows, :] = vln
        for g in range(N_SGU_GROUPS):
            cs = slice(g * gw, (g + 1) * gw)
            ws = jnp.where(cc <= rr, ws_ref[g, :cl, :cl], 0.0).astype(BF16)
            sv = _dot(ws, vln[:, cs].astype(BF16)) + bst_ref[:cl, g:g + 1]
            o_ref[0, rows, cs] = (u_ref[0, rows, cs] * sv * gt_ref[0, rows, cs]).astype(BF16)


def _sgu_call(zact, g_v, b_v, w_s, b_s_t, lb, cl, want_v):
    b, l, cw3 = zact.shape
    cw = cw3 // 3
    out_spec = pl.BlockSpec((1, lb, cw), lambda bi, r: (bi, r, 0))
    out_shape = [jax.ShapeDtypeStruct((b, l, cw), BF16)]
    out_specs = [out_spec]
    if want_v:
        out_shape.append(jax.ShapeDtypeStruct((b, l, cw), F32))
        out_specs.append(out_spec)
    vec = pl.BlockSpec((1, cw), lambda bi, r: (0, 0))
    return pl.pallas_call(
        functools.partial(_sgu_body, cl=cl),
        grid=(b, l // lb),
        in_specs=[pl.BlockSpec((1, lb, cw), lambda bi, r: (bi, r, 0)),
                  pl.BlockSpec((1, lb, cw), lambda bi, r: (bi, r, 1)),
                  pl.BlockSpec((1, lb, cw), lambda bi, r: (bi, r, 2)),
                  vec, vec,
                  pl.BlockSpec(w_s.shape, lambda bi, r: (0, 0, 0)),
                  pl.BlockSpec(b_s_t.shape, lambda bi, r: (0, 0))],
        out_specs=out_specs,
        out_shape=out_shape,
        compiler_params=_params(("arbitrary", "arbitrary"), 48),
        name="sgu",
    )(zact, zact, zact, g_v.reshape(1, cw), b_v.reshape(1, cw), w_s, b_s_t)


def _row_blocking(b, l, rows=1024):
    if l >= rows:
        return 1, rows
    return rows // l, l


def _layer_ab(x, shift, scale, gate, g_norm, wts, hist, pos0, cache):
    b, l, d = x.shape
    aw = wts["cols"]["q"][1]
    assert all(c[0] % 8 == 0 for c in wts["cols"].values())
    nh = aw // HEAD_DIM
    bb, lb = _row_blocking(b, l)
    h = _normmod_call(x, g_norm, scale, shift, *_row_blocking(b, l, 256)).reshape(b * l, d)

    col = wts["cols"]
    proj = functools.partial(_proj_call, h, wts["win_t"], layer=wts["idx"], w_transposed=True)
    (qn,) = proj("headnorm", [BF16], extra=wts["gq"], cols=col["q"], name="proj_q")
    k32, k16 = proj("headnorm", [F32, BF16], extra=wts["gk"], cols=col["k"], name="proj_k")
    v32, v16 = proj("plain", [F32, BF16], cols=col["v"], name="proj_v")
    (logf,) = proj("logsigmoid", [F32], extra=wts["bf"], cols=col["f"], name="proj_f")
    (u,) = proj("plain", [F32], cols=col["u"], name="proj_u")
    (sgate,) = proj("silu", [BF16], cols=col["g"], name="proj_gate")

    logf = logf.reshape(b, l, nh)
    logf_t = logf.transpose(0, 2, 1).reshape(b * nh, l)
    fcs = _cumsum_call(logf_t, False, min(b * nh, 64))
    frow = fcs.reshape(b, nh, l)
    fcol = frow.transpose(0, 2, 1)

    q3, k3, v3 = (t.reshape(b, l, aw) for t in (qn, k16, v16))
    sgate3 = sgate.reshape(b, l, -1)
    if cache is None:
        mixed_a = _attn_prompt_call(q3, k3, v3, fcol, frow, sgate3)
    else:
        ck, cv, clogf, layer = cache
        p = ck.shape[2]
        clf_t = clogf.transpose(0, 2, 1).reshape(b * nh, p)
        gpast = _cumsum_call(clf_t, True, 64).reshape(b, nh, p)
        mixed_a = _attn_sample_call(q3, k3, v3, ck, cv, layer, fcol, frow, gpast, sgate3)

    u3 = u.reshape(b, l, -1)
    hist16 = jnp.pad(hist, ((0, 0), (HALO - POOL_HIST, 0), (0, 0)))
    mixed_b = _pool_call(u3, hist16, sgate3, wts["wpool"], wts["lspool"], pos0, min(l, 256))

    y = _outproj_call([mixed_a, mixed_b], wts["wo"], wts["idx"], x, gate, bb, lb)
    if l >= POOL_HIST:
        new_hist = u3[:, l - POOL_HIST:]
    else:
        new_hist = jnp.concatenate([hist, u3], axis=1)[:, -POOL_HIST:]
    return (y, k32.reshape(b, l, nh, HEAD_DIM), v32.reshape(b, l, nh, HEAD_DIM), logf, new_hist)


def _layer_c(x, shift, scale, gate, g_norm, wts, want_v):
    b, l, d = x.shape
    bb, lb = _row_blocking(b, l)
    h = _normmod_call(x, g_norm, scale, shift, *_row_blocking(b, l, 256)).reshape(b * l, d)
    cw = wts["wo"].shape[1]
    tn = 512
    (zact,) = _proj_call(h, wts["win"], "gelu_silu", [BF16], n_gelu_tiles=2 * cw // tn, tn=tn,
                         layer=wts["idx"], name="proj_c")
    cl = min(l, SGU_CHUNK)
    outs = _sgu_call(zact.reshape(b, l, 3 * cw), wts["gv"], wts["bv"], wts["ws"], wts["bst"],
                     min(l, 2 * SGU_CHUNK), cl, want_v)
    y = _outproj_call([outs[0]], wts["wo"], wts["idx"], x, gate, bb, lb)
    return y, (outs[1] if want_v else None)


def kernel(x_prompt, x_sample, cache_k, cache_v, cache_logf, state_pool, c_prompt, c_sample,
           w_ada, b_ada, g_norm, w_in_ab, b_forget, g_q, g_k, w_pool, ls_pool, w_out_ab,
           w_in_c, g_v, b_v, w_s, b_s, w_out_c):
    bp, sp, d = x_prompt.shape
    bs = x_sample.shape[0]
    depth = w_ada.shape[0]
    nh = cache_k.shape[3]
    aw = nh * HEAD_DIM
    bw = w_pool.shape[2] * w_pool.shape[1]
    past_len = cache_k.shape[2]

    c_all = jnp.concatenate([c_prompt, c_sample], axis=0)
    c_all = jnp.pad(c_all, ((0, -c_all.shape[0] % 8), (0, 0)))
    mod = _ada_call(c_all, w_ada, b_ada)

    def mods(layer, lo, n):
        m = mod[layer, lo:lo + n].reshape(n, 1, 3 * d)
        return m[..., :d], m[..., d:2 * d], m[..., 2 * d:]

    yp, ys = x_prompt, x_sample
    outs_p, outs_s, sgu_v = [], [], []
    for layer in range(depth):
        i = layer // 2
        shp, scp, gp = mods(layer, 0, bp)
        shs, scs, gs = mods(layer, bp, bs)
        if layer % 2 == 0:
            o_f, o_u, o_g = 3 * aw, 3 * aw + nh, 3 * aw + nh + bw
            wts = {
                "win_t": jnp.swapaxes(w_in_ab, 1, 2),
                "cols": {"q": (0, aw), "k": (aw, aw), "v": (2 * aw, aw), "f": (o_f, nh), "u": (o_u, bw),
                         "g": (o_g, w_in_ab.shape[2] - o_g)},
                "bf": b_forget[i].reshape(1, nh),
                "gq": g_q[i].reshape(1, HEAD_DIM),
                "gk": g_k[i].reshape(1, HEAD_DIM),
                "wpool": w_pool[i].astype(BF16),
                "lspool": ls_pool[i],
                "wo": w_out_ab, "idx": i,
            }
            zero_hist = jnp.zeros((bp, POOL_HIST, bw), F32)
            rp = _layer_ab(yp, shp, scp, gp, g_norm[layer], wts, zero_hist, 0, None)
            rs = _layer_ab(ys, shs, scs, gs, g_norm[layer], wts, state_pool[i], past_len,
                           (cache_k, cache_v, cache_logf[i], i))
            yp, ys = rp[0], rs[0]
            outs_p.append(rp[1:])
            outs_s.append(rs[1:])
        else:
            wts = {
                "win": w_in_c, "wo": w_out_c, "idx": i,
                "gv": g_v[i], "bv": b_v[i],
                "ws": w_s[i], "bst": b_s[i].T,
            }
            yp, _ = _layer_c(yp, shp, scp, gp, g_norm[layer], wts, False)
            ys, v_c = _layer_c(ys, shs, scs, gs, g_norm[layer], wts, True)
            sgu_v.append(v_c)

    def stack(group, idx):
        return jnp.stack([o[idx] for o in group])

    return (yp, ys,
            stack(outs_p, 0), stack(outs_p, 1), stack(outs_p, 2), stack(outs_p, 3),
            stack(outs_s, 0), stack(outs_s, 1), stack(outs_s, 2), stack(outs_s, 3),
            jnp.stack(sgu_v))
```

```python
import functools

import jax
import jax.numpy as jnp
from jax import lax
from jax.experimental import pallas as pl
from jax.experimental.pallas import tpu as pltpu

F32 = jnp.float32
BF16 = jnp.bfloat16

EPS = 1e-6
HEAD_DIM = 128
POOL_WINDOWS = (2, 4, 8, 16)
POOL_HIST = max(POOL_WINDOWS) - 1
HALO = POOL_HIST + 1
SGU_CHUNK = 128
N_SGU_GROUPS = 16
NEG = -1e30
LOG2E = 1.4426950408889634
LANE = 128
ROWS_ELEMENTWISE = 512
MIB = 1024 * 1024


def _params(sem, vmem_mib):
    return pltpu.CompilerParams(dimension_semantics=sem, vmem_limit_bytes=vmem_mib * MIB)


def _dot(a, b):
    return jnp.dot(a, b, preferred_element_type=F32)


def _dot_nt(a, b):
    return lax.dot_general(a, b, (((1,), (1,)), ((), ())), preferred_element_type=F32)


def _ada_body(c_ref, w_ref, b_ref, o_ref):
    a = jax.nn.silu(c_ref[...]).astype(BF16)
    o_ref[0] = _dot(a, w_ref[0].astype(BF16)) + b_ref[0]


def _ada_call(c_all, w_ada, b_ada, tn=512):
    depth, d, n = w_ada.shape
    rp = c_all.shape[0]
    return pl.pallas_call(
        _ada_body,
        grid=(depth, n // tn),
        in_specs=[pl.BlockSpec((rp, d), lambda l, j: (0, 0)),
                  pl.BlockSpec((1, d, tn), lambda l, j: (l, 0, j)),
                  pl.BlockSpec((1, 1, tn), lambda l, j: (l, 0, j))],
        out_specs=pl.BlockSpec((1, rp, tn), lambda l, j: (l, 0, j)),
        out_shape=jax.ShapeDtypeStruct((depth, rp, n), F32),
        compiler_params=_params(("arbitrary", "arbitrary"), 40),
        name="ada_mod",
    )(c_all, w_ada, b_ada.reshape(depth, 1, n))


def _normmod_body(x_ref, g_ref, sc_ref, sh_ref, o_ref):
    x = x_ref[...]
    y = x * lax.rsqrt(jnp.mean(x * x, axis=-1, keepdims=True) + EPS) * g_ref[...]
    o_ref[...] = (y * (1 + sc_ref[...]) + sh_ref[...]).astype(BF16)


def _normmod_call(x, g, scale, shift, bb, lb):
    b, l, d = x.shape
    row = pl.BlockSpec((bb, lb, d), lambda i, r: (i, r, 0))
    per_b = pl.BlockSpec((bb, 1, d), lambda i, r: (i, 0, 0))
    return pl.pallas_call(
        _normmod_body,
        grid=(b // bb, l // lb),
        in_specs=[row, pl.BlockSpec((1, 1, d), lambda i, r: (0, 0, 0)), per_b, per_b],
        out_specs=row,
        out_shape=jax.ShapeDtypeStruct((b, l, d), BF16),
        compiler_params=_params(("arbitrary", "arbitrary"), 40),
        name="norm_mod",
    )(x, g.reshape(1, 1, d), scale, shift)


def _proj_body(a_ref, w_ref, *rest, mode, n_gelu_tiles, w_transposed):
    w = w_ref[...].astype(BF16)
    acc = _dot_nt(a_ref[...], w) if w_transposed else _dot(a_ref[...], w)
    tn = acc.shape[1]
    if mode == "headnorm":
        g_ref, outs = rest[0], rest[1:]
        for hh in range(tn // HEAD_DIM):
            cs = slice(hh * HEAD_DIM, (hh + 1) * HEAD_DIM)
            blk = acc[:, cs]
            y = blk * lax.rsqrt(jnp.mean(blk * blk, axis=-1, keepdims=True) + EPS) * g_ref[...]
            for o_ref in outs:
                o_ref[:, cs] = y.astype(o_ref.dtype)
    elif mode == "plain":
        for o_ref in rest:
            o_ref[...] = acc.astype(o_ref.dtype)
    elif mode == "logsigmoid":
        b_ref, o_ref = rest
        x = acc + b_ref[...]
        o_ref[...] = jnp.minimum(x, 0.0) - jnp.log1p(jnp.exp(-jnp.abs(x)))
    elif mode == "silu":
        (o_ref,) = rest
        o_ref[...] = jax.nn.silu(acc).astype(o_ref.dtype)
    elif mode == "gelu_silu":
        (o_ref,) = rest
        j = pl.program_id(1)

        @pl.when(j < n_gelu_tiles)
        def _():
            o_ref[...] = (0.5 * acc * (1.0 + lax.erf(acc * (2.0 ** -0.5)))).astype(o_ref.dtype)

        @pl.when(j >= n_gelu_tiles)
        def _():
            o_ref[...] = jax.nn.silu(acc).astype(o_ref.dtype)
    else:
        raise ValueError(mode)


def _proj_call(a, w, mode, out_dtypes, extra=None, n_gelu_tiles=0, tm=1024, tn=512, layer=0, cols=None,
               w_transposed=False, name="proj"):
    m, k = a.shape
    col0, n = cols if cols is not None else (0, w.shape[1 if w_transposed else 2])
    tn = min(tn, n)
    assert n % tn == 0
    if w_transposed:
        sub = 8
        assert col0 % sub == 0 and tn % sub == 0
        wspec = pl.BlockSpec((None, pl.Element(tn), pl.Element(k)),
                             lambda i, j: (layer, (col0 // sub + j * (tn // sub)) * sub, 0))
    else:
        j0 = col0 // tn
        assert j0 * tn == col0
        wspec = pl.BlockSpec((None, k, tn), lambda i, j: (layer, 0, j + j0))
    in_specs = [pl.BlockSpec((tm, k), lambda i, j: (i, 0)), wspec]
    args = [a, w]
    if extra is not None:
        ew = extra.shape[1]
        if ew == n:
            in_specs.append(pl.BlockSpec((1, tn), lambda i, j: (0, j)))
        else:
            in_specs.append(pl.BlockSpec((1, ew), lambda i, j: (0, 0)))
        args.append(extra)
    out_spec = pl.BlockSpec((tm, tn), lambda i, j: (i, j))
    outs = pl.pallas_call(
        functools.partial(_proj_body, mode=mode, n_gelu_tiles=n_gelu_tiles, w_transposed=w_transposed),
        grid=(m // tm, n // tn),
        in_specs=in_specs,
        out_specs=[out_spec] * len(out_dtypes),
        out_shape=[jax.ShapeDtypeStruct((m, n), dt) for dt in out_dtypes],
        compiler_params=_params(("arbitrary", "arbitrary"), 48),
        name=name,
    )(*args)
    return outs


def _outproj_body(*refs, n_a):
    a_refs, w_refs = refs[:n_a], refs[n_a:2 * n_a]
    x_ref, g_ref, o_ref = refs[2 * n_a:]
    bb, lb, tn = x_ref.shape
    acc = None
    for a_ref, w_ref in zip(a_refs, w_refs):
        a = a_ref[...].reshape(bb * lb, a_ref.shape[2])
        d = _dot(a, w_ref[...].astype(BF16))
        acc = d if acc is None else acc + d
    o_ref[...] = x_ref[...] + g_ref[...] * acc.reshape(bb, lb, tn)


def _outproj_call(a_list, w, layer, x, gate, bb, lb, tn=512):
    b, l, n = x.shape
    n_a = len(a_list)
    in_specs = []
    for a in a_list:
        in_specs.append(pl.BlockSpec((bb, lb, a.shape[2]), lambda i, r, j: (i, r, 0)))
    for ai, a in enumerate(a_list):
        in_specs.append(pl.BlockSpec((None, a.shape[2], tn), lambda i, r, j, ai=ai: (layer, ai, j)))
    xspec = pl.BlockSpec((bb, lb, tn), lambda i, r, j: (i, r, j))
    in_specs += [xspec, pl.BlockSpec((bb, 1, tn), lambda i, r, j: (i, 0, j))]
    return pl.pallas_call(
        functools.partial(_outproj_body, n_a=n_a),
        grid=(b // bb, l // lb, n // tn),
        in_specs=in_specs,
        out_specs=xspec,
        out_shape=jax.ShapeDtypeStruct((b, l, n), F32),
        compiler_params=_params(("arbitrary", "arbitrary", "arbitrary"), 48),
        name="out_proj",
    )(*a_list, *([w] * n_a), x, gate)


def _cumsum_body(x_ref, o_ref, *, ch, reverse_exclusive):
    rb, n = x_ref.shape
    ii = lax.broadcasted_iota(jnp.int32, (ch, ch), 0)
    jj = lax.broadcasted_iota(jnp.int32, (ch, ch), 1)
    tri = (ii > jj) if reverse_exclusive else (ii <= jj)
    tri = tri.astype(F32)
    chunks = range(n // ch)
    carry = jnp.zeros((rb, 1), F32)
    for c in (reversed(chunks) if reverse_exclusive else chunks):
        xc = x_ref[:, c * ch:(c + 1) * ch]
        y = jnp.dot(xc, tri, precision=lax.Precision.HIGHEST, preferred_element_type=F32)
        o_ref[:, c * ch:(c + 1) * ch] = y + carry
        carry = carry + jnp.sum(xc, axis=-1, keepdims=True)


def _cumsum_call(x, reverse_exclusive, rb):
    r, n = x.shape
    ch = min(n, 512)
    return pl.pallas_call(
        functools.partial(_cumsum_body, ch=ch, reverse_exclusive=reverse_exclusive),
        grid=(r // rb,),
        in_specs=[pl.BlockSpec((rb, n), lambda i: (i, 0))],
        out_specs=pl.BlockSpec((rb, n), lambda i: (i, 0)),
        out_shape=jax.ShapeDtypeStruct((r, n), F32),
        compiler_params=_params(("arbitrary",), 32),
        name="cumsum",
    )(x)


def _store_scores(raw, h, c1, qb_ref, kbias, mask, s_ref, pm_ref):
    tk = raw.shape[1]
    w = min(tk, LANE)
    pm = None
    for c in range(tk // w):
        cs = slice(c * w, (c + 1) * w)
        sc = raw[:, cs] * c1 + (qb_ref[h, :, :w] + kbias[:, cs])
        if mask is not None:
            sc = jnp.where(mask[:, cs], sc, NEG)
        s_ref[h, :, cs] = sc
        pm = sc if pm is None else jnp.maximum(pm, sc)
    pm_ref[h] = pm


def _softmax_update(s_ref, p_ref, pm_ref, m_ref, a_ref):
    nslots, _, tk = s_ref.shape
    w = pm_ref.shape[2]
    for h in range(nslots):
        m_prev = m_ref[h]
        m_new = jnp.maximum(m_prev, jnp.max(pm_ref[h], axis=-1, keepdims=True))
        a_ref[h] = jnp.exp2(m_prev - m_new)
        m_ref[h] = m_new
    for h in range(nslots):
        for c in range(tk // w):
            cs = slice(c * w, (c + 1) * w)
            p_ref[h, :, cs] = jnp.exp2(s_ref[h, :, cs] - m_ref[h, :, :w]).astype(BF16)


def _accumulate(h, p, v, a_ref, l_ref, acc_ref):
    res = _dot(p, jnp.concatenate([v, jnp.ones_like(v)], axis=1))
    a = a_ref[h]
    return a * acc_ref[h] + res[:, :HEAD_DIM], a * l_ref[h] + res[:, HEAD_DIM:]


def _causal_mask(rows, cols):
    rr = lax.broadcasted_iota(jnp.int32, (rows, cols), 0)
    cc = lax.broadcasted_iota(jnp.int32, (rows, cols), 1)
    return cc <= rr


def _init_softmax_state(m_ref, l_ref, acc_ref):
    m_ref[...] = jnp.full(m_ref.shape, NEG, F32)
    l_ref[...] = jnp.zeros(l_ref.shape, F32)
    acc_ref[...] = jnp.zeros(acc_ref.shape, F32)


def _head_cols(h):
    return slice(h * HEAD_DIM, (h + 1) * HEAD_DIM)


def _prompt_iter(tbl_ref, t, q_ref, k_ref, v_ref, fcol_ref, frow_ref, sg_ref, o_ref,
                 s_ref, p_ref, pm_ref, m_ref, l_ref, a_ref, acc_ref, qb_ref, *, tq, hg, c1):
    g, qi, kj = tbl_ref[1, t], tbl_ref[2, t], tbl_ref[3, t]

    @pl.when(kj == 0)
    def _():
        fblk = fcol_ref[0] * LOG2E
        lane = lax.broadcasted_iota(jnp.int32, fblk.shape, 1)
        for hh in range(hg):
            fc = jnp.sum(jnp.where(lane == g * hg + hh, fblk, 0.0), axis=-1, keepdims=True)
            qb_ref[hh] = jnp.broadcast_to(fc, qb_ref.shape[1:])
        _init_softmax_state(m_ref, l_ref, acc_ref)

    def step(masked):
        rows = pl.ds(pl.multiple_of(kj * tq, tq), tq)
        mask = _causal_mask(tq, tq) if masked else None
        for hh in range(hg):
            raw = _dot_nt(q_ref[0, :, _head_cols(hh)], k_ref[0, rows, _head_cols(hh)])
            kbias = frow_ref[0, kj, hh:hh + 1, :] * -LOG2E
            _store_scores(raw, hh, c1, qb_ref, kbias, mask, s_ref, pm_ref)
        _softmax_update(s_ref, p_ref, pm_ref, m_ref, a_ref)
        for hh in range(hg):
            acc_ref[hh], l_ref[hh] = _accumulate(hh, p_ref[hh], v_ref[0, rows, _head_cols(hh)],
                                                 a_ref, l_ref, acc_ref)

    @pl.when(kj < qi)
    def _():
        step(False)

    @pl.when(kj == qi)
    def _():
        step(True)
        for hh in range(hg):
            cs = _head_cols(hh)
            o_ref[0, :, cs] = (acc_ref[hh] / l_ref[hh] * sg_ref[0, :, cs]).astype(BF16)


def _cache_copy(hbm_ref, buf_ref, sem_ref, layer, bi, pi, slot, h):
    tp = buf_ref.shape[2]
    return pltpu.make_async_copy(hbm_ref.at[layer, bi, pl.ds(pi * tp, tp), h, :],
                                 buf_ref.at[slot, h], sem_ref.at[slot, h])


def _sample_step(t, q_ref, kn_ref, vn_ref, ck_hbm, cv_hbm, fcol_ref, frow_ref, gp_ref, sg_ref, o_ref,
                 kbuf, vbuf, ksem, vsem,
                 s_ref, p_ref, pm_ref, sn_ref, pn_ref, pmn_ref, m_ref, l_ref, a_ref, acc_ref,
                 qb_ref, *, n_p, n_steps, nh, c1, layer):
    bi = lax.div(t, n_p)
    pi = lax.rem(t, n_p)
    slot = lax.rem(t, 2)
    l = q_ref.shape[1]

    def fetch(b_to, p_to, slot_to):
        for h in range(nh):
            _cache_copy(ck_hbm, kbuf, ksem, layer, b_to, p_to, slot_to, h).start()
            _cache_copy(cv_hbm, vbuf, vsem, layer, b_to, p_to, slot_to, h).start()

    @pl.when(t == 0)
    def _():
        fetch(0, 0, 0)

    @pl.when(t + 1 < n_steps)
    def _():
        fetch(lax.div(t + 1, n_p), lax.rem(t + 1, n_p), 1 - slot)

    @pl.when(pi == 0)
    def _():
        _init_softmax_state(m_ref, l_ref, acc_ref)
        for h in range(nh):
            qb_ref[h] = jnp.broadcast_to(fcol_ref[0, :, h:h + 1] * LOG2E, qb_ref.shape[1:])

    for h in range(nh):
        _cache_copy(ck_hbm, kbuf, ksem, layer, bi, pi, slot, h).wait()
        _cache_copy(cv_hbm, vbuf, vsem, layer, bi, pi, slot, h).wait()
    for h in range(nh):
        raw = _dot_nt(q_ref[0, :, _head_cols(h)], kbuf[slot, h].astype(BF16))
        _store_scores(raw, h, c1, qb_ref, gp_ref[0, h:h + 1, :] * LOG2E, None, s_ref, pm_ref)
    _softmax_update(s_ref, p_ref, pm_ref, m_ref, a_ref)
    for h in range(nh):
        acc_ref[h], l_ref[h] = _accumulate(h, p_ref[h], vbuf[slot, h].astype(BF16), a_ref, l_ref, acc_ref)

    @pl.when(pi == n_p - 1)
    def _():
        mask = _causal_mask(l, l)
        for h in range(nh):
            cs = _head_cols(h)
            raw = _dot_nt(q_ref[0, :, cs], kn_ref[0, :, cs])
            _store_scores(raw, h, c1, qb_ref, frow_ref[0, h:h + 1, :] * -LOG2E, mask, sn_ref, pmn_ref)
        _softmax_update(sn_ref, pn_ref, pmn_ref, m_ref, a_ref)
        for h in range(nh):
            cs = _head_cols(h)
            acc, den = _accumulate(h, pn_ref[h], vn_ref[0, :, cs], a_ref, l_ref, acc_ref)
            o_ref[0, :, cs] = (acc / den * sg_ref[0, :, cs]).astype(BF16)


N_PROMPT_IN, N_SAMPLE_IN, N_PROMPT_SCRATCH = 6, 9, 8


def _attn_body(tbl_ref, *refs, n_prompt, n_sample, n_p, tq, hg, nh, c1, layer):
    t = pl.program_id(0)
    p_in = refs[:N_PROMPT_IN]
    s_in = refs[N_PROMPT_IN:N_PROMPT_IN + N_SAMPLE_IN]
    op_ref, os_ref = refs[N_PROMPT_IN + N_SAMPLE_IN:N_PROMPT_IN + N_SAMPLE_IN + 2]
    scratch = refs[N_PROMPT_IN + N_SAMPLE_IN + 2:]
    p_scr, s_scr = scratch[:N_PROMPT_SCRATCH], scratch[N_PROMPT_SCRATCH:]

    @pl.when(t < n_sample)
    def _():
        _sample_step(t, *s_in, os_ref, *s_scr, n_p=n_p, n_steps=n_sample, nh=nh, c1=c1, layer=layer)

    @pl.when(t < n_prompt)
    def _():
        _prompt_iter(tbl_ref, t, *p_in, op_ref, *p_scr, tq=tq, hg=hg, c1=c1)


def _attn_call(pr, sm, ck, cv, layer, tq=512, hg=4, tp=512):
    bp, s, aw = pr["q"].shape
    bs, l, _ = sm["q"].shape
    nh = aw // HEAD_DIM
    ng, nq, gw = nh // hg, s // tq, hg * HEAD_DIM
    n_p = ck.shape[2] // tp
    sched = [(b, g, qi, kj) for b in range(bp) for g in range(ng) for qi in range(nq) for kj in range(qi + 1)]
    n_prompt, n_sample = len(sched), bs * n_p
    n_steps = max(n_prompt, n_sample)
    sched += [sched[-1]] * (n_steps - n_prompt)
    tbl = jnp.asarray(sched, jnp.int32).T

    qspec = pl.BlockSpec((1, tq, gw), lambda t, tb: (tb[0, t], tb[2, t], tb[1, t]))
    kvspec = pl.BlockSpec((1, s, gw), lambda t, tb: (tb[0, t], 0, tb[1, t]), pipeline_mode=pl.Buffered(1))
    frow4 = pr["frow"].reshape(bp * ng, hg, nq, tq).transpose(0, 2, 1, 3)
    prompt_specs = [qspec, kvspec, kvspec,
                    pl.BlockSpec((1, tq, nh), lambda t, tb: (tb[0, t], tb[2, t], 0)),
                    pl.BlockSpec((1, nq, hg, tq), lambda t, tb: (tb[0, t] * ng + tb[1, t], 0, 0, 0)),
                    qspec]

    def sb(t):
        return jnp.minimum(t // n_p, bs - 1)

    new = pl.BlockSpec((1, l, aw), lambda t, tb: (sb(t), 0, 0))
    cache = pl.BlockSpec(memory_space=pl.ANY)
    sample_specs = [new, new, new, cache, cache,
                    pl.BlockSpec((1, l, nh), lambda t, tb: (sb(t), 0, 0)),
                    pl.BlockSpec((1, nh, l), lambda t, tb: (sb(t), 0, 0)),
                    pl.BlockSpec((1, nh, tp), lambda t, tb: (sb(t), 0, jnp.where(t < n_sample, t % n_p, n_p - 1))),
                    new]

    prep = pltpu.VMEM((hg, tq, LANE), F32)
    prompt_scratch = [pltpu.VMEM((hg, tq, tq), F32), pltpu.VMEM((hg, tq, tq), BF16)] + [prep] * 6
    cbuf = pltpu.VMEM((2, nh, tp, HEAD_DIM), ck.dtype)
    csem = pltpu.SemaphoreType.DMA((2, nh))
    srep = pltpu.VMEM((nh, l, LANE), F32)
    sample_scratch = [cbuf, cbuf, csem, csem,
                      pltpu.VMEM((nh, l, tp), F32), pltpu.VMEM((nh, l, tp), BF16), srep,
                      pltpu.VMEM((nh, l, l), F32), pltpu.VMEM((nh, l, l), BF16),
                      pltpu.VMEM((nh, l, min(l, LANE)), F32)] + [srep] * 5
    assert len(prompt_specs) == N_PROMPT_IN and len(sample_specs) == N_SAMPLE_IN
    assert len(prompt_scratch) == N_PROMPT_SCRATCH

    return pl.pallas_call(
        functools.partial(_attn_body, n_prompt=n_prompt, n_sample=n_sample, n_p=n_p, tq=tq, hg=hg, nh=nh,
                          c1=HEAD_DIM ** -0.5 * LOG2E, layer=layer),
        grid_spec=pltpu.PrefetchScalarGridSpec(
            num_scalar_prefetch=1, grid=(n_steps,),
            in_specs=prompt_specs + sample_specs,
            out_specs=[qspec, new],
            scratch_shapes=prompt_scratch + sample_scratch),
        out_shape=[jax.ShapeDtypeStruct((bp, s, aw), BF16), jax.ShapeDtypeStruct((bs, l, aw), BF16)],
        compiler_params=_params(("arbitrary",), 56),
        name="attn",
    )(tbl, pr["q"], pr["k"], pr["v"], pr["fcol"], frow4, pr["sgate"],
      sm["q"], sm["k"], sm["v"], ck, cv, sm["fcol"], sm["frow"], sm["gpast"], sm["sgate"])


def _pool_body(u_ref, halo_ref, hist_ref, sg_ref, w_ref, ls_ref, o_ref, ext_ref, *, pos0):
    r = pl.program_id(1)
    tm = u_ref.shape[1]
    group = u_ref.shape[2] // len(POOL_WINDOWS)
    ext_ref[HALO:HALO + tm, :] = u_ref[0]

    @pl.when(r == 0)
    def _():
        ext_ref[0:HALO, :] = hist_ref[0]

    @pl.when(r > 0)
    def _():
        ext_ref[0:HALO, :] = halo_ref[0]

    n_before = lax.broadcasted_iota(jnp.int32, (tm, 1), 0) + (pos0 + 1) + r * tm
    for gi, w in enumerate(POOL_WINDOWS):
        cs = slice(gi * group, (gi + 1) * group)
        win = ext_ref[HALO:HALO + tm, cs]
        for i in range(1, w):
            win = win + ext_ref[HALO - i:HALO - i + tm, cs]
        cnt = jnp.minimum(w, n_before).astype(F32)
        d = win / cnt - u_ref[0, :, cs]
        y = _dot(d.astype(BF16), w_ref[gi]) * ls_ref[:, cs]
        o_ref[0, :, cs] = (y * sg_ref[0, :, cs]).astype(BF16)


def _pool_call(u, hist16, sgate, w_pool, ls_pool, pos0, tm):
    b, l, bw = u.shape
    g = w_pool.shape[0]
    row = pl.BlockSpec((1, tm, bw), lambda bi, r: (bi, r, 0))
    halo_blocks = tm // HALO
    return pl.pallas_call(
        functools.partial(_pool_body, pos0=pos0),
        grid=(b, l // tm),
        in_specs=[row,
                  pl.BlockSpec((1, HALO, bw), lambda bi, r: (bi, jnp.maximum(r * halo_blocks - 1, 0), 0)),
                  pl.BlockSpec((1, HALO, bw), lambda bi, r: (bi, 0, 0)),
                  pl.BlockSpec((1, tm, bw), lambda bi, r: (bi, r, 1)),
                  pl.BlockSpec((g, bw // g, bw // g), lambda bi, r: (0, 0, 0)),
                  pl.BlockSpec((1, bw), lambda bi, r: (0, 0))],
        out_specs=row,
        out_shape=jax.ShapeDtypeStruct((b, l, bw), BF16),
        scratch_shapes=[pltpu.VMEM((HALO + tm, bw), F32)],
        compiler_params=_params(("arbitrary", "arbitrary"), 40),
        name="pool_mix",
    )(u, u, hist16, sgate, w_pool, ls_pool.reshape(1, bw))


def _sgu_body(u_ref, v_ref, gt_ref, gv_ref, bv_ref, ws_ref, bst_ref, o_ref, *vout, cl):
    lb, cw = u_ref.shape[1], u_ref.shape[2]
    gw = cw // N_SGU_GROUPS
    rr = lax.broadcasted_iota(jnp.int32, (cl, cl), 0)
    cc = lax.broadcasted_iota(jnp.int32, (cl, cl), 1)
    for c in range(lb // cl):
        rows = slice(c * cl, (c + 1) * cl)
        v = v_ref[0, rows, :].astype(F32)
        xc = v - jnp.mean(v, axis=-1, keepdims=True)
        var = jnp.mean(xc * xc, axis=-1, keepdims=True)
        vln = xc * lax.rsqrt(var + EPS) * gv_ref[...] + bv_ref[...]
        if vout:
            vout[0][0, rows, :] = vln
        for g in range(N_SGU_GROUPS):
            cs = slice(g * gw, (g + 1) * gw)
            ws = jnp.where(cc <= rr, ws_ref[g, :cl, :cl], 0.0).astype(BF16)
            sv = _dot(ws, vln[:, cs].astype(BF16)) + bst_ref[:cl, g:g + 1]
            o_ref[0, rows, cs] = (u_ref[0, rows, cs] * sv * gt_ref[0, rows, cs]).astype(BF16)


def _sgu_call(zact, g_v, b_v, w_s, b_s_t, lb, cl, want_v):
    b, l, cw3 = zact.shape
    cw = cw3 // 3
    out_spec = pl.BlockSpec((1, lb, cw), lambda bi, r: (bi, r, 0))
    out_shape = [jax.ShapeDtypeStruct((b, l, cw), BF16)]
    out_specs = [out_spec]
    if want_v:
        out_shape.append(jax.ShapeDtypeStruct((b, l, cw), F32))
        out_specs.append(out_spec)
    vec = pl.BlockSpec((1, cw), lambda bi, r: (0, 0))
    return pl.pallas_call(
        functools.partial(_sgu_body, cl=cl),
        grid=(b, l // lb),
        in_specs=[pl.BlockSpec((1, lb, cw), lambda bi, r: (bi, r, 0)),
                  pl.BlockSpec((1, lb, cw), lambda bi, r: (bi, r, 1)),
                  pl.BlockSpec((1, lb, cw), lambda bi, r: (bi, r, 2)),
                  vec, vec,
                  pl.BlockSpec(w_s.shape, lambda bi, r: (0, 0, 0)),
                  pl.BlockSpec(b_s_t.shape, lambda bi, r: (0, 0))],
        out_specs=out_specs,
        out_shape=out_shape,
        compiler_params=_params(("arbitrary", "arbitrary"), 48),
        name="sgu",
    )(zact, zact, zact, g_v.reshape(1, cw), b_v.reshape(1, cw), w_s, b_s_t)


def _row_blocking(b, l, rows=1024):
    if l >= rows:
        return 1, rows
    return rows // l, l


def _layer_ab_pre(x, shift, scale, g_norm, wts, clogf):
    b, l, d = x.shape
    col = wts["cols"]
    aw = col["q"][1]
    assert all(c[0] % 8 == 0 for c in col.values())
    nh = aw // HEAD_DIM
    h = _normmod_call(x, g_norm, scale, shift, *_row_blocking(b, l, ROWS_ELEMENTWISE)).reshape(b * l, d)

    proj = functools.partial(_proj_call, h, wts["win_t"], layer=wts["idx"], w_transposed=True)
    (qn,) = proj("headnorm", [BF16], extra=wts["gq"], cols=col["q"], name="proj_q")
    k32, k16 = proj("headnorm", [F32, BF16], extra=wts["gk"], cols=col["k"], name="proj_k")
    v32, v16 = proj("plain", [F32, BF16], cols=col["v"], name="proj_v")
    (logf,) = proj("logsigmoid", [F32], extra=wts["bf"], cols=col["f"], name="proj_f")
    (u,) = proj("plain", [F32], cols=col["u"], name="proj_u")
    (sgate,) = proj("silu", [BF16], cols=col["g"], name="proj_gate")

    logf = logf.reshape(b, l, nh)
    logf_t = logf.transpose(0, 2, 1).reshape(b * nh, l)
    frow = _cumsum_call(logf_t, False, min(b * nh, 64)).reshape(b, nh, l)
    grp = {"q": qn.reshape(b, l, aw), "k": k16.reshape(b, l, aw), "v": v16.reshape(b, l, aw),
           "frow": frow, "fcol": frow.transpose(0, 2, 1), "sgate": sgate.reshape(b, l, -1),
           "u": u.reshape(b, l, -1), "logf": logf,
           "k32": k32.reshape(b, l, nh, HEAD_DIM), "v32": v32.reshape(b, l, nh, HEAD_DIM)}
    if clogf is not None:
        p = clogf.shape[1]
        clf_t = clogf.transpose(0, 2, 1).reshape(b * nh, p)
        grp["gpast"] = _cumsum_call(clf_t, True, 64).reshape(b, nh, p)
    return grp


def _layer_ab_post(x, gate, grp, mixed_a, wts, hist, pos0):
    b, l, _ = x.shape
    bb, lb = _row_blocking(b, l)
    u3 = grp["u"]
    hist16 = jnp.pad(hist, ((0, 0), (HALO - POOL_HIST, 0), (0, 0)))
    mixed_b = _pool_call(u3, hist16, grp["sgate"], wts["wpool"], wts["lspool"], pos0, min(l, ROWS_ELEMENTWISE))
    y = _outproj_call([mixed_a, mixed_b], wts["wo"], wts["idx"], x, gate, bb, lb)
    if l >= POOL_HIST:
        new_hist = u3[:, l - POOL_HIST:]
    else:
        new_hist = jnp.concatenate([hist, u3], axis=1)[:, -POOL_HIST:]
    return y, grp["k32"], grp["v32"], grp["logf"], new_hist


def _layer_c(x, shift, scale, gate, g_norm, wts, want_v):
    b, l, d = x.shape
    bb, lb = _row_blocking(b, l)
    h = _normmod_call(x, g_norm, scale, shift, *_row_blocking(b, l, ROWS_ELEMENTWISE)).reshape(b * l, d)
    cw = wts["wo"].shape[1]
    tn = 512
    (zact,) = _proj_call(h, wts["win"], "gelu_silu", [BF16], n_gelu_tiles=2 * cw // tn, tn=tn,
                         layer=wts["idx"], name="proj_c")
    cl = min(l, SGU_CHUNK)
    outs = _sgu_call(zact.reshape(b, l, 3 * cw), wts["gv"], wts["bv"], wts["ws"], wts["bst"],
                     min(l, ROWS_ELEMENTWISE), cl, want_v)
    y = _outproj_call([outs[0]], wts["wo"], wts["idx"], x, gate, bb, lb)
    return y, (outs[1] if want_v else None)


def kernel(x_prompt, x_sample, cache_k, cache_v, cache_logf, state_pool, c_prompt, c_sample,
           w_ada, b_ada, g_norm, w_in_ab, b_forget, g_q, g_k, w_pool, ls_pool, w_out_ab,
           w_in_c, g_v, b_v, w_s, b_s, w_out_c):
    bp, sp, d = x_prompt.shape
    bs = x_sample.shape[0]
    depth = w_ada.shape[0]
    nh = cache_k.shape[3]
    aw = nh * HEAD_DIM
    bw = w_pool.shape[2] * w_pool.shape[1]
    past_len = cache_k.shape[2]

    c_all = jnp.concatenate([c_prompt, c_sample], axis=0)
    c_all = jnp.pad(c_all, ((0, -c_all.shape[0] % 8), (0, 0)))
    mod = _ada_call(c_all, w_ada, b_ada)

    def mods(layer, lo, n):
        m = mod[layer, lo:lo + n].reshape(n, 1, 3 * d)
        return m[..., :d], m[..., d:2 * d], m[..., 2 * d:]

    yp, ys = x_prompt, x_sample
    outs_p, outs_s, sgu_v = [], [], []
    for layer in range(depth):
        i = layer // 2
        shp, scp, gp = mods(layer, 0, bp)
        shs, scs, gs = mods(layer, bp, bs)
        if layer % 2 == 0:
            o_f, o_u, o_g = 3 * aw, 3 * aw + nh, 3 * aw + nh + bw
            wts = {
                "win_t": jnp.swapaxes(w_in_ab, 1, 2),
                "cols": {"q": (0, aw), "k": (aw, aw), "v": (2 * aw, aw), "f": (o_f, nh), "u": (o_u, bw),
                         "g": (o_g, w_in_ab.shape[2] - o_g)},
                "bf": b_forget[i].reshape(1, nh),
                "gq": g_q[i].reshape(1, HEAD_DIM),
                "gk": g_k[i].reshape(1, HEAD_DIM),
                "wpool": w_pool[i].astype(BF16),
                "lspool": ls_pool[i],
                "wo": w_out_ab, "idx": i,
            }
            zero_hist = jnp.zeros((bp, POOL_HIST, bw), F32)
            grp_p = _layer_ab_pre(yp, shp, scp, g_norm[layer], wts, None)
            grp_s = _layer_ab_pre(ys, shs, scs, g_norm[layer], wts, cache_logf[i])
            mixed_p, mixed_s = _attn_call(grp_p, grp_s, cache_k, cache_v, i)
            rp = _layer_ab_post(yp, gp, grp_p, mixed_p, wts, zero_hist, 0)
            rs = _layer_ab_post(ys, gs, grp_s, mixed_s, wts, state_pool[i], past_len)
            yp, ys = rp[0], rs[0]
            outs_p.append(rp[1:])
            outs_s.append(rs[1:])
        else:
            wts = {
                "win": w_in_c, "wo": w_out_c, "idx": i,
                "gv": g_v[i], "bv": b_v[i],
                "ws": w_s[i], "bst": b_s[i].T,
            }
            yp, _ = _layer_c(yp, shp, scp, gp, g_norm[layer], wts, False)
            ys, v_c = _layer_c(ys, shs, scs, gs, g_norm[layer], wts, True)
            sgu_v.append(v_c)

    def stack(group, idx):
        return jnp.stack([o[idx] for o in group])

    return (yp, ys,
            stack(outs_p, 0), stack(outs_p, 1), stack(outs_p, 2), stack(outs_p, 3),
            stack(outs_s, 0), stack(outs_s, 1), stack(outs_s, 2), stack(outs_s, 3),
            jnp.stack(sgu_v))
```

```python
import functools

import jax
import jax.numpy as jnp
from jax import lax
from jax.experimental import pallas as pl
from jax.experimental.pallas import tpu as pltpu

F32 = jnp.float32
BF16 = jnp.bfloat16

EPS = 1e-6
HEAD_DIM = 128
POOL_WINDOWS = (2, 4, 8, 16)
POOL_HIST = max(POOL_WINDOWS) - 1
HALO = POOL_HIST + 1
SGU_CHUNK = 128
N_SGU_GROUPS = 16
NEG = -1e30
LOG2E = 1.4426950408889634
LANE = 128
ROWS_ELEMENTWISE = 512
MAX_SINGLE_BLOCK_ROWS = 2048
MIB = 1024 * 1024


def _params(sem, vmem_mib):
    return pltpu.CompilerParams(dimension_semantics=sem, vmem_limit_bytes=vmem_mib * MIB)


def _dot(a, b):
    return jnp.dot(a, b, preferred_element_type=F32)


def _dot_nt(a, b):
    return lax.dot_general(a, b, (((1,), (1,)), ((), ())), preferred_element_type=F32)


def _ada_body(c_ref, w_ref, b_ref, o_ref):
    a = jax.nn.silu(c_ref[...]).astype(BF16)
    o_ref[0] = _dot(a, w_ref[0].astype(BF16)) + b_ref[0]


def _ada_call(c_all, w_ada, b_ada, tn=512):
    depth, d, n = w_ada.shape
    rp = c_all.shape[0]
    return pl.pallas_call(
        _ada_body,
        grid=(depth, n // tn),
        in_specs=[pl.BlockSpec((rp, d), lambda l, j: (0, 0)),
                  pl.BlockSpec((1, d, tn), lambda l, j: (l, 0, j)),
                  pl.BlockSpec((1, 1, tn), lambda l, j: (l, 0, j))],
        out_specs=pl.BlockSpec((1, rp, tn), lambda l, j: (l, 0, j)),
        out_shape=jax.ShapeDtypeStruct((depth, rp, n), F32),
        compiler_params=_params(("arbitrary", "arbitrary"), 40),
        name="ada_mod",
    )(c_all, w_ada, b_ada.reshape(depth, 1, n))


def _normmod_body(x_ref, g_ref, sc_ref, sh_ref, o_ref):
    x = x_ref[...]
    y = x * lax.rsqrt(jnp.mean(x * x, axis=-1, keepdims=True) + EPS) * g_ref[...]
    o_ref[...] = (y * (1 + sc_ref[...]) + sh_ref[...]).astype(BF16)


def _normmod_call(x, g, scale, shift, bb, lb):
    b, l, d = x.shape
    row = pl.BlockSpec((bb, lb, d), lambda i, r: (i, r, 0))
    per_b = pl.BlockSpec((bb, 1, d), lambda i, r: (i, 0, 0))
    return pl.pallas_call(
        _normmod_body,
        grid=(b // bb, l // lb),
        in_specs=[row, pl.BlockSpec((1, 1, d), lambda i, r: (0, 0, 0)), per_b, per_b],
        out_specs=row,
        out_shape=jax.ShapeDtypeStruct((b, l, d), BF16),
        compiler_params=_params(("arbitrary", "arbitrary"), 40),
        name="norm_mod",
    )(x, g.reshape(1, 1, d), scale, shift)


def _proj_body(a_ref, w_ref, *rest, mode, n_gelu_tiles, w_transposed):
    w = w_ref[...].astype(BF16)
    acc = _dot_nt(a_ref[...], w) if w_transposed else _dot(a_ref[...], w)
    tn = acc.shape[1]
    if mode == "headnorm":
        g_ref, outs = rest[0], rest[1:]
        for hh in range(tn // HEAD_DIM):
            cs = slice(hh * HEAD_DIM, (hh + 1) * HEAD_DIM)
            blk = acc[:, cs]
            y = blk * lax.rsqrt(jnp.mean(blk * blk, axis=-1, keepdims=True) + EPS) * g_ref[...]
            for o_ref in outs:
                o_ref[:, cs] = y.astype(o_ref.dtype)
    elif mode == "plain":
        for o_ref in rest:
            o_ref[...] = acc.astype(o_ref.dtype)
    elif mode == "logsigmoid":
        b_ref, o_ref = rest
        x = acc + b_ref[...]
        o_ref[...] = jnp.minimum(x, 0.0) - jnp.log1p(jnp.exp(-jnp.abs(x)))
    elif mode == "silu":
        (o_ref,) = rest
        o_ref[...] = jax.nn.silu(acc).astype(o_ref.dtype)
    elif mode == "gelu_silu":
        (o_ref,) = rest
        j = pl.program_id(1)

        @pl.when(j < n_gelu_tiles)
        def _():
            o_ref[...] = (0.5 * acc * (1.0 + lax.erf(acc * (2.0 ** -0.5)))).astype(o_ref.dtype)

        @pl.when(j >= n_gelu_tiles)
        def _():
            o_ref[...] = jax.nn.silu(acc).astype(o_ref.dtype)
    else:
        raise ValueError(mode)


def _proj_call(a, w, mode, out_dtypes, extra=None, n_gelu_tiles=0, tm=1024, tn=512, layer=0, cols=None,
               w_transposed=False, name="proj"):
    m, k = a.shape
    col0, n = cols if cols is not None else (0, w.shape[1 if w_transposed else 2])
    tn = min(tn, n)
    assert n % tn == 0
    if w_transposed:
        sub = 8
        assert col0 % sub == 0 and tn % sub == 0
        wspec = pl.BlockSpec((None, pl.Element(tn), pl.Element(k)),
                             lambda i, j: (layer, (col0 // sub + j * (tn // sub)) * sub, 0))
    else:
        j0 = col0 // tn
        assert j0 * tn == col0
        wspec = pl.BlockSpec((None, k, tn), lambda i, j: (layer, 0, j + j0))
    if m <= MAX_SINGLE_BLOCK_ROWS:
        tm = m
    if m // tm == 1:
        aspec = pl.BlockSpec((tm, k), lambda i, j: (i, 0), pipeline_mode=pl.Buffered(1))
    else:
        aspec = pl.BlockSpec((tm, k), lambda i, j: (i, 0))
    in_specs = [aspec, wspec]
    args = [a, w]
    if extra is not None:
        ew = extra.shape[1]
        if ew == n:
            in_specs.append(pl.BlockSpec((1, tn), lambda i, j: (0, j)))
        else:
            in_specs.append(pl.BlockSpec((1, ew), lambda i, j: (0, 0)))
        args.append(extra)
    out_spec = pl.BlockSpec((tm, tn), lambda i, j: (i, j))
    outs = pl.pallas_call(
        functools.partial(_proj_body, mode=mode, n_gelu_tiles=n_gelu_tiles, w_transposed=w_transposed),
        grid=(m // tm, n // tn),
        in_specs=in_specs,
        out_specs=[out_spec] * len(out_dtypes),
        out_shape=[jax.ShapeDtypeStruct((m, n), dt) for dt in out_dtypes],
        compiler_params=_params(("arbitrary", "arbitrary"), 56),
        name=name,
    )(*args)
    return outs


def _outproj_body(*refs, n_a):
    a_refs, w_refs = refs[:n_a], refs[n_a:2 * n_a]
    x_ref, g_ref, o_ref = refs[2 * n_a:]
    bb, lb, tn = x_ref.shape
    acc = None
    for a_ref, w_ref in zip(a_refs, w_refs):
        a = a_ref[...].reshape(bb * lb, a_ref.shape[2])
        d = _dot(a, w_ref[...].astype(BF16))
        acc = d if acc is None else acc + d
    o_ref[...] = x_ref[...] + g_ref[...] * acc.reshape(bb, lb, tn)


def _outproj_call(a_list, w, layer, x, gate, bb, lb, tn=512):
    b, l, n = x.shape
    n_a = len(a_list)
    in_specs = []
    for a in a_list:
        in_specs.append(pl.BlockSpec((bb, lb, a.shape[2]), lambda i, r, j: (i, r, 0)))
    for ai, a in enumerate(a_list):
        in_specs.append(pl.BlockSpec((None, a.shape[2], tn), lambda i, r, j, ai=ai: (layer, ai, j)))
    xspec = pl.BlockSpec((bb, lb, tn), lambda i, r, j: (i, r, j))
    in_specs += [xspec, pl.BlockSpec((bb, 1, tn), lambda i, r, j: (i, 0, j))]
    return pl.pallas_call(
        functools.partial(_outproj_body, n_a=n_a),
        grid=(b // bb, l // lb, n // tn),
        in_specs=in_specs,
        out_specs=xspec,
        out_shape=jax.ShapeDtypeStruct((b, l, n), F32),
        compiler_params=_params(("arbitrary", "arbitrary", "arbitrary"), 48),
        name="out_proj",
    )(*a_list, *([w] * n_a), x, gate)


def _cumsum_body(x_ref, o_ref, *, ch, reverse_exclusive):
    rb, n = x_ref.shape
    ii = lax.broadcasted_iota(jnp.int32, (ch, ch), 0)
    jj = lax.broadcasted_iota(jnp.int32, (ch, ch), 1)
    tri = (ii > jj) if reverse_exclusive else (ii <= jj)
    tri = tri.astype(F32)
    chunks = range(n // ch)
    carry = jnp.zeros((rb, 1), F32)
    for c in (reversed(chunks) if reverse_exclusive else chunks):
        xc = x_ref[:, c * ch:(c + 1) * ch]
        y = jnp.dot(xc, tri, precision=lax.Precision.HIGHEST, preferred_element_type=F32)
        o_ref[:, c * ch:(c + 1) * ch] = y + carry
        carry = carry + jnp.sum(xc, axis=-1, keepdims=True)


def _cumsum_call(x, reverse_exclusive, rb):
    r, n = x.shape
    ch = min(n, 512)
    return pl.pallas_call(
        functools.partial(_cumsum_body, ch=ch, reverse_exclusive=reverse_exclusive),
        grid=(r // rb,),
        in_specs=[pl.BlockSpec((rb, n), lambda i: (i, 0))],
        out_specs=pl.BlockSpec((rb, n), lambda i: (i, 0)),
        out_shape=jax.ShapeDtypeStruct((r, n), F32),
        compiler_params=_params(("arbitrary",), 32),
        name="cumsum",
    )(x)


def _store_scores(raw, h, c1, qb_ref, kbias, mask, s_ref, pm_ref):
    tk = raw.shape[1]
    w = min(tk, LANE)
    pm = None
    for c in range(tk // w):
        cs = slice(c * w, (c + 1) * w)
        sc = raw[:, cs] * c1 + (qb_ref[h, :, :w] + kbias[:, cs])
        if mask is not None:
            sc = jnp.where(mask[:, cs], sc, NEG)
        s_ref[h, :, cs] = sc
        pm = sc if pm is None else jnp.maximum(pm, sc)
    pm_ref[h] = pm


def _softmax_update(s_ref, p_ref, pm_ref, m_ref, a_ref):
    nslots, _, tk = s_ref.shape
    w = pm_ref.shape[2]
    for h in range(nslots):
        m_prev = m_ref[h]
        m_new = jnp.maximum(m_prev, jnp.max(pm_ref[h], axis=-1, keepdims=True))
        a_ref[h] = jnp.exp2(m_prev - m_new)
        m_ref[h] = m_new
    for h in range(nslots):
        for c in range(tk // w):
            cs = slice(c * w, (c + 1) * w)
            p_ref[h, :, cs] = jnp.exp2(s_ref[h, :, cs] - m_ref[h, :, :w]).astype(BF16)


def _accumulate(h, p, v, a_ref, l_ref, acc_ref):
    res = _dot(p, jnp.concatenate([v, jnp.ones_like(v)], axis=1))
    a = a_ref[h]
    return a * acc_ref[h] + res[:, :HEAD_DIM], a * l_ref[h] + res[:, HEAD_DIM:]


def _causal_mask(rows, cols):
    rr = lax.broadcasted_iota(jnp.int32, (rows, cols), 0)
    cc = lax.broadcasted_iota(jnp.int32, (rows, cols), 1)
    return cc <= rr


def _init_softmax_state(m_ref, l_ref, acc_ref):
    m_ref[...] = jnp.full(m_ref.shape, NEG, F32)
    l_ref[...] = jnp.zeros(l_ref.shape, F32)
    acc_ref[...] = jnp.zeros(acc_ref.shape, F32)


def _head_cols(h):
    return slice(h * HEAD_DIM, (h + 1) * HEAD_DIM)


def _prompt_iter(tbl_ref, t, q_ref, k_ref, v_ref, fcol_ref, frow_ref, sg_ref, o_ref,
                 s_ref, p_ref, pm_ref, m_ref, l_ref, a_ref, acc_ref, qb_ref, *, tq, hg, c1):
    g, qi, kj = tbl_ref[1, t], tbl_ref[2, t], tbl_ref[3, t]

    @pl.when(kj == 0)
    def _():
        fblk = fcol_ref[0] * LOG2E
        lane = lax.broadcasted_iota(jnp.int32, fblk.shape, 1)
        for hh in range(hg):
            fc = jnp.sum(jnp.where(lane == g * hg + hh, fblk, 0.0), axis=-1, keepdims=True)
            qb_ref[hh] = jnp.broadcast_to(fc, qb_ref.shape[1:])
        _init_softmax_state(m_ref, l_ref, acc_ref)

    def step(masked):
        rows = pl.ds(pl.multiple_of(kj * tq, tq), tq)
        mask = _causal_mask(tq, tq) if masked else None
        for hh in range(hg):
            raw = _dot_nt(q_ref[0, :, _head_cols(hh)], k_ref[0, rows, _head_cols(hh)])
            kbias = frow_ref[0, kj, hh:hh + 1, :] * -LOG2E
            _store_scores(raw, hh, c1, qb_ref, kbias, mask, s_ref, pm_ref)
        _softmax_update(s_ref, p_ref, pm_ref, m_ref, a_ref)
        for hh in range(hg):
            acc_ref[hh], l_ref[hh] = _accumulate(hh, p_ref[hh], v_ref[0, rows, _head_cols(hh)],
                                                 a_ref, l_ref, acc_ref)

    @pl.when(kj < qi)
    def _():
        step(False)

    @pl.when(kj == qi)
    def _():
        step(True)
        for hh in range(hg):
            cs = _head_cols(hh)
            o_ref[0, :, cs] = (acc_ref[hh] / l_ref[hh] * sg_ref[0, :, cs]).astype(BF16)


def _cache_copy(hbm_ref, buf_ref, sem_ref, layer, bi, pi, slot, h):
    tp = buf_ref.shape[2]
    return pltpu.make_async_copy(hbm_ref.at[layer, bi, pl.ds(pi * tp, tp), h, :],
                                 buf_ref.at[slot, h], sem_ref.at[slot, h])


def _sample_step(t, q_ref, kn_ref, vn_ref, ck_hbm, cv_hbm, fcol_ref, frow_ref, gp_ref, sg_ref, o_ref,
                 kbuf, vbuf, ksem, vsem,
                 s_ref, p_ref, pm_ref, sn_ref, pn_ref, pmn_ref, m_ref, l_ref, a_ref, acc_ref,
                 qb_ref, *, n_p, n_steps, nh, c1, layer):
    bi = lax.div(t, n_p)
    pi = lax.rem(t, n_p)
    slot = lax.rem(t, 2)
    l = q_ref.shape[1]

    def fetch(b_to, p_to, slot_to):
        for h in range(nh):
            _cache_copy(ck_hbm, kbuf, ksem, layer, b_to, p_to, slot_to, h).start()
            _cache_copy(cv_hbm, vbuf, vsem, layer, b_to, p_to, slot_to, h).start()

    @pl.when(t == 0)
    def _():
        fetch(0, 0, 0)

    @pl.when(t + 1 < n_steps)
    def _():
        fetch(lax.div(t + 1, n_p), lax.rem(t + 1, n_p), 1 - slot)

    @pl.when(pi == 0)
    def _():
        _init_softmax_state(m_ref, l_ref, acc_ref)
        for h in range(nh):
            qb_ref[h] = jnp.broadcast_to(fcol_ref[0, :, h:h + 1] * LOG2E, qb_ref.shape[1:])

    for h in range(nh):
        _cache_copy(ck_hbm, kbuf, ksem, layer, bi, pi, slot, h).wait()
        _cache_copy(cv_hbm, vbuf, vsem, layer, bi, pi, slot, h).wait()
    for h in range(nh):
        raw = _dot_nt(q_ref[0, :, _head_cols(h)], kbuf[slot, h].astype(BF16))
        _store_scores(raw, h, c1, qb_ref, gp_ref[0, h:h + 1, :] * LOG2E, None, s_ref, pm_ref)
    _softmax_update(s_ref, p_ref, pm_ref, m_ref, a_ref)
    for h in range(nh):
        acc_ref[h], l_ref[h] = _accumulate(h, p_ref[h], vbuf[slot, h].astype(BF16), a_ref, l_ref, acc_ref)

    @pl.when(pi == n_p - 1)
    def _():
        mask = _causal_mask(l, l)
        for h in range(nh):
            cs = _head_cols(h)
            raw = _dot_nt(q_ref[0, :, cs], kn_ref[0, :, cs])
            _store_scores(raw, h, c1, qb_ref, frow_ref[0, h:h + 1, :] * -LOG2E, mask, sn_ref, pmn_ref)
        _softmax_update(sn_ref, pn_ref, pmn_ref, m_ref, a_ref)
        for h in range(nh):
            cs = _head_cols(h)
            acc, den = _accumulate(h, pn_ref[h], vn_ref[0, :, cs], a_ref, l_ref, acc_ref)
            o_ref[0, :, cs] = (acc / den * sg_ref[0, :, cs]).astype(BF16)


N_PROMPT_IN, N_SAMPLE_IN, N_PROMPT_SCRATCH = 6, 9, 8


def _attn_body(tbl_ref, *refs, n_prompt, n_sample, n_p, tq, hg, nh, c1, layer):
    t = pl.program_id(0)
    p_in = refs[:N_PROMPT_IN]
    s_in = refs[N_PROMPT_IN:N_PROMPT_IN + N_SAMPLE_IN]
    op_ref, os_ref = refs[N_PROMPT_IN + N_SAMPLE_IN:N_PROMPT_IN + N_SAMPLE_IN + 2]
    scratch = refs[N_PROMPT_IN + N_SAMPLE_IN + 2:]
    p_scr, s_scr = scratch[:N_PROMPT_SCRATCH], scratch[N_PROMPT_SCRATCH:]

    @pl.when(t < n_sample)
    def _():
        _sample_step(t, *s_in, os_ref, *s_scr, n_p=n_p, n_steps=n_sample, nh=nh, c1=c1, layer=layer)

    @pl.when(t < n_prompt)
    def _():
        _prompt_iter(tbl_ref, t, *p_in, op_ref, *p_scr, tq=tq, hg=hg, c1=c1)


def _attn_call(pr, sm, ck, cv, layer, tq=512, hg=4, tp=512):
    bp, s, aw = pr["q"].shape
    bs, l, _ = sm["q"].shape
    nh = aw // HEAD_DIM
    ng, nq, gw = nh // hg, s // tq, hg * HEAD_DIM
    n_p = ck.shape[2] // tp
    sched = [(b, g, qi, kj) for b in range(bp) for g in range(ng) for qi in range(nq) for kj in range(qi + 1)]
    n_prompt, n_sample = len(sched), bs * n_p
    n_steps = max(n_prompt, n_sample)
    sched += [sched[-1]] * (n_steps - n_prompt)
    tbl = jnp.asarray(sched, jnp.int32).T

    qspec = pl.BlockSpec((1, tq, gw), lambda t, tb: (tb[0, t], tb[2, t], tb[1, t]))
    kvspec = pl.BlockSpec((1, s, gw), lambda t, tb: (tb[0, t], 0, tb[1, t]), pipeline_mode=pl.Buffered(1))
    frow4 = pr["frow"].reshape(bp * ng, hg, nq, tq).transpose(0, 2, 1, 3)
    prompt_specs = [qspec, kvspec, kvspec,
                    pl.BlockSpec((1, tq, nh), lambda t, tb: (tb[0, t], tb[2, t], 0)),
                    pl.BlockSpec((1, nq, hg, tq), lambda t, tb: (tb[0, t] * ng + tb[1, t], 0, 0, 0)),
                    qspec]

    def sb(t):
        return jnp.minimum(t // n_p, bs - 1)

    new = pl.BlockSpec((1, l, aw), lambda t, tb: (sb(t), 0, 0))
    cache = pl.BlockSpec(memory_space=pl.ANY)
    sample_specs = [new, new, new, cache, cache,
                    pl.BlockSpec((1, l, nh), lambda t, tb: (sb(t), 0, 0)),
                    pl.BlockSpec((1, nh, l), lambda t, tb: (sb(t), 0, 0)),
                    pl.BlockSpec((1, nh, tp), lambda t, tb: (sb(t), 0, jnp.where(t < n_sample, t % n_p, n_p - 1))),
                    new]

    prep = pltpu.VMEM((hg, tq, LANE), F32)
    prompt_scratch = [pltpu.VMEM((hg, tq, tq), F32), pltpu.VMEM((hg, tq, tq), BF16)] + [prep] * 6
    cbuf = pltpu.VMEM((2, nh, tp, HEAD_DIM), ck.dtype)
    csem = pltpu.SemaphoreType.DMA((2, nh))
    srep = pltpu.VMEM((nh, l, LANE), F32)
    sample_scratch = [cbuf, cbuf, csem, csem,
                      pltpu.VMEM((nh, l, tp), F32), pltpu.VMEM((nh, l, tp), BF16), srep,
                      pltpu.VMEM((nh, l, l), F32), pltpu.VMEM((nh, l, l), BF16),
                      pltpu.VMEM((nh, l, min(l, LANE)), F32)] + [srep] * 5
    assert len(prompt_specs) == N_PROMPT_IN and len(sample_specs) == N_SAMPLE_IN
    assert len(prompt_scratch) == N_PROMPT_SCRATCH

    return pl.pallas_call(
        functools.partial(_attn_body, n_prompt=n_prompt, n_sample=n_sample, n_p=n_p, tq=tq, hg=hg, nh=nh,
                          c1=HEAD_DIM ** -0.5 * LOG2E, layer=layer),
        grid_spec=pltpu.PrefetchScalarGridSpec(
            num_scalar_prefetch=1, grid=(n_steps,),
            in_specs=prompt_specs + sample_specs,
            out_specs=[qspec, new],
            scratch_shapes=prompt_scratch + sample_scratch),
        out_shape=[jax.ShapeDtypeStruct((bp, s, aw), BF16), jax.ShapeDtypeStruct((bs, l, aw), BF16)],
        compiler_params=_params(("arbitrary",), 56),
        name="attn",
    )(tbl, pr["q"], pr["k"], pr["v"], pr["fcol"], frow4, pr["sgate"],
      sm["q"], sm["k"], sm["v"], ck, cv, sm["fcol"], sm["frow"], sm["gpast"], sm["sgate"])


def _pool_body(u_ref, halo_ref, hist_ref, sg_ref, w_ref, ls_ref, o_ref, ext_ref, *, pos0):
    r = pl.program_id(1)
    tm = u_ref.shape[1]
    group = u_ref.shape[2] // len(POOL_WINDOWS)
    ext_ref[HALO:HALO + tm, :] = u_ref[0]

    @pl.when(r == 0)
    def _():
        ext_ref[0:HALO, :] = hist_ref[0]

    @pl.when(r > 0)
    def _():
        ext_ref[0:HALO, :] = halo_ref[0]

    n_before = lax.broadcasted_iota(jnp.int32, (tm, 1), 0) + (pos0 + 1) + r * tm
    for gi, w in enumerate(POOL_WINDOWS):
        cs = slice(gi * group, (gi + 1) * group)
        win = ext_ref[HALO:HALO + tm, cs]
        for i in range(1, w):
            win = win + ext_ref[HALO - i:HALO - i + tm, cs]
        cnt = jnp.minimum(w, n_before).astype(F32)
        d = win / cnt - u_ref[0, :, cs]
        y = _dot(d.astype(BF16), w_ref[gi]) * ls_ref[:, cs]
        o_ref[0, :, cs] = (y * sg_ref[0, :, cs]).astype(BF16)


def _pool_call(u, hist16, sgate, w_pool, ls_pool, pos0, tm):
    b, l, bw = u.shape
    g = w_pool.shape[0]
    row = pl.BlockSpec((1, tm, bw), lambda bi, r: (bi, r, 0))
    halo_blocks = tm // HALO
    return pl.pallas_call(
        functools.partial(_pool_body, pos0=pos0),
        grid=(b, l // tm),
        in_specs=[row,
                  pl.BlockSpec((1, HALO, bw), lambda bi, r: (bi, jnp.maximum(r * halo_blocks - 1, 0), 0)),
                  pl.BlockSpec((1, HALO, bw), lambda bi, r: (bi, 0, 0)),
                  pl.BlockSpec((1, tm, bw), lambda bi, r: (bi, r, 1)),
                  pl.BlockSpec((g, bw // g, bw // g), lambda bi, r: (0, 0, 0)),
                  pl.BlockSpec((1, bw), lambda bi, r: (0, 0))],
        out_specs=row,
        out_shape=jax.ShapeDtypeStruct((b, l, bw), BF16),
        scratch_shapes=[pltpu.VMEM((HALO + tm, bw), F32)],
        compiler_params=_params(("arbitrary", "arbitrary"), 40),
        name="pool_mix",
    )(u, u, hist16, sgate, w_pool, ls_pool.reshape(1, bw))


def _sgu_body(u_ref, v_ref, gt_ref, gv_ref, bv_ref, ws_ref, bst_ref, o_ref, *vout, cl):
    lb, cw = u_ref.shape[1], u_ref.shape[2]
    gw = cw // N_SGU_GROUPS
    rr = lax.broadcasted_iota(jnp.int32, (cl, cl), 0)
    cc = lax.broadcasted_iota(jnp.int32, (cl, cl), 1)
    for c in range(lb // cl):
        rows = slice(c * cl, (c + 1) * cl)
        v = v_ref[0, rows, :].astype(F32)
        xc = v - jnp.mean(v, axis=-1, keepdims=True)
        var = jnp.mean(xc * xc, axis=-1, keepdims=True)
        vln = xc * lax.rsqrt(var + EPS) * gv_ref[...] + bv_ref[...]
        if vout:
            vout[0][0, rows, :] = vln
        for g in range(N_SGU_GROUPS):
            cs = slice(g * gw, (g + 1) * gw)
            ws = jnp.where(cc <= rr, ws_ref[g, :cl, :cl], 0.0).astype(BF16)
            sv = _dot(ws, vln[:, cs].astype(BF16)) + bst_ref[:cl, g:g + 1]
            o_ref[0, rows, cs] = (u_ref[0, rows, cs] * sv * gt_ref[0, rows, cs]).astype(BF16)


def _sgu_call(zact, g_v, b_v, w_s, b_s_t, lb, cl, want_v):
    b, l, cw3 = zact.shape
    cw = cw3 // 3
    out_spec = pl.BlockSpec((1, lb, cw), lambda bi, r: (bi, r, 0))
    out_shape = [jax.ShapeDtypeStruct((b, l, cw), BF16)]
    out_specs = [out_spec]
    if want_v:
        out_shape.append(jax.ShapeDtypeStruct((b, l, cw), F32))
        out_specs.append(out_spec)
    vec = pl.BlockSpec((1, cw), lambda bi, r: (0, 0))
    return pl.pallas_call(
        functools.partial(_sgu_body, cl=cl),
        grid=(b, l // lb),
        in_specs=[pl.BlockSpec((1, lb, cw), lambda bi, r: (bi, r, 0)),
                  pl.BlockSpec((1, lb, cw), lambda bi, r: (bi, r, 1)),
                  pl.BlockSpec((1, lb, cw), lambda bi, r: (bi, r, 2)),
                  vec, vec,
                  pl.BlockSpec(w_s.shape, lambda bi, r: (0, 0, 0)),
                  pl.BlockSpec(b_s_t.shape, lambda bi, r: (0, 0))],
        out_specs=out_specs,
        out_shape=out_shape,
        compiler_params=_params(("arbitrary", "arbitrary"), 48),
        name="sgu",
    )(zact, zact, zact, g_v.reshape(1, cw), b_v.reshape(1, cw), w_s, b_s_t)


def _row_blocking(b, l, rows=1024):
    if l >= rows:
        return 1, rows
    return rows // l, l


def _layer_ab_pre(x, shift, scale, g_norm, wts, clogf):
    b, l, d = x.shape
    col = wts["cols"]
    aw = col["q"][1]
    assert all(c[0] % 8 == 0 for c in col.values())
    nh = aw // HEAD_DIM
    h = _normmod_call(x, g_norm, scale, shift, *_row_blocking(b, l, ROWS_ELEMENTWISE)).reshape(b * l, d)

    proj = functools.partial(_proj_call, h, wts["win_t"], layer=wts["idx"], w_transposed=True)
    (qn,) = proj("headnorm", [BF16], extra=wts["gq"], cols=col["q"], name="proj_q")
    k32, k16 = proj("headnorm", [F32, BF16], extra=wts["gk"], cols=col["k"], name="proj_k")
    v32, v16 = proj("plain", [F32, BF16], cols=col["v"], name="proj_v")
    (logf,) = proj("logsigmoid", [F32], extra=wts["bf"], cols=col["f"], name="proj_f")
    (u,) = proj("plain", [F32], cols=col["u"], name="proj_u")
    (sgate,) = proj("silu", [BF16], cols=col["g"], name="proj_gate")

    logf = logf.reshape(b, l, nh)
    logf_t = logf.transpose(0, 2, 1).reshape(b * nh, l)
    frow = _cumsum_call(logf_t, False, min(b * nh, 64)).reshape(b, nh, l)
    grp = {"q": qn.reshape(b, l, aw), "k": k16.reshape(b, l, aw), "v": v16.reshape(b, l, aw),
           "frow": frow, "fcol": frow.transpose(0, 2, 1), "sgate": sgate.reshape(b, l, -1),
           "u": u.reshape(b, l, -1), "logf": logf,
           "k32": k32.reshape(b, l, nh, HEAD_DIM), "v32": v32.reshape(b, l, nh, HEAD_DIM)}
    if clogf is not None:
        p = clogf.shape[1]
        clf_t = clogf.transpose(0, 2, 1).reshape(b * nh, p)
        grp["gpast"] = _cumsum_call(clf_t, True, 64).reshape(b, nh, p)
    return grp


def _layer_ab_post(x, gate, grp, mixed_a, wts, hist, pos0):
    b, l, _ = x.shape
    bb, lb = _row_blocking(b, l)
    u3 = grp["u"]
    hist16 = jnp.pad(hist, ((0, 0), (HALO - POOL_HIST, 0), (0, 0)))
    mixed_b = _pool_call(u3, hist16, grp["sgate"], wts["wpool"], wts["lspool"], pos0, min(l, ROWS_ELEMENTWISE))
    y = _outproj_call([mixed_a, mixed_b], wts["wo"], wts["idx"], x, gate, bb, lb)
    if l >= POOL_HIST:
        new_hist = u3[:, l - POOL_HIST:]
    else:
        new_hist = jnp.concatenate([hist, u3], axis=1)[:, -POOL_HIST:]
    return y, grp["k32"], grp["v32"], grp["logf"], new_hist


def _layer_c(x, shift, scale, gate, g_norm, wts, want_v):
    b, l, d = x.shape
    bb, lb = _row_blocking(b, l)
    h = _normmod_call(x, g_norm, scale, shift, *_row_blocking(b, l, ROWS_ELEMENTWISE)).reshape(b * l, d)
    cw = wts["wo"].shape[1]
    tn = 512
    (zact,) = _proj_call(h, wts["win"], "gelu_silu", [BF16], n_gelu_tiles=2 * cw // tn, tn=tn,
                         layer=wts["idx"], name="proj_c")
    cl = min(l, SGU_CHUNK)
    outs = _sgu_call(zact.reshape(b, l, 3 * cw), wts["gv"], wts["bv"], wts["ws"], wts["bst"],
                     min(l, ROWS_ELEMENTWISE), cl, want_v)
    y = _outproj_call([outs[0]], wts["wo"], wts["idx"], x, gate, bb, lb)
    return y, (outs[1] if want_v else None)


def kernel(x_prompt, x_sample, cache_k, cache_v, cache_logf, state_pool, c_prompt, c_sample,
           w_ada, b_ada, g_norm, w_in_ab, b_forget, g_q, g_k, w_pool, ls_pool, w_out_ab,
           w_in_c, g_v, b_v, w_s, b_s, w_out_c):
    bp, sp, d = x_prompt.shape
    bs = x_sample.shape[0]
    depth = w_ada.shape[0]
    nh = cache_k.shape[3]
    aw = nh * HEAD_DIM
    bw = w_pool.shape[2] * w_pool.shape[1]
    past_len = cache_k.shape[2]

    c_all = jnp.concatenate([c_prompt, c_sample], axis=0)
    c_all = jnp.pad(c_all, ((0, -c_all.shape[0] % 8), (0, 0)))
    mod = _ada_call(c_all, w_ada, b_ada)

    def mods(layer, lo, n):
        m = mod[layer, lo:lo + n].reshape(n, 1, 3 * d)
        return m[..., :d], m[..., d:2 * d], m[..., 2 * d:]

    yp, ys = x_prompt, x_sample
    outs_p, outs_s, sgu_v = [], [], []
    for layer in range(depth):
        i = layer // 2
        shp, scp, gp = mods(layer, 0, bp)
        shs, scs, gs = mods(layer, bp, bs)
        if layer % 2 == 0:
            o_f, o_u, o_g = 3 * aw, 3 * aw + nh, 3 * aw + nh + bw
            wts = {
                "win_t": jnp.swapaxes(w_in_ab, 1, 2),
                "cols": {"q": (0, aw), "k": (aw, aw), "v": (2 * aw, aw), "f": (o_f, nh), "u": (o_u, bw),
                         "g": (o_g, w_in_ab.shape[2] - o_g)},
                "bf": b_forget[i].reshape(1, nh),
                "gq": g_q[i].reshape(1, HEAD_DIM),
                "gk": g_k[i].reshape(1, HEAD_DIM),
                "wpool": w_pool[i].astype(BF16),
                "lspool": ls_pool[i],
                "wo": w_out_ab, "idx": i,
            }
            zero_hist = jnp.zeros((bp, POOL_HIST, bw), F32)
            grp_p = _layer_ab_pre(yp, shp, scp, g_norm[layer], wts, None)
            grp_s = _layer_ab_pre(ys, shs, scs, g_norm[layer], wts, cache_logf[i])
            mixed_p, mixed_s = _attn_call(grp_p, grp_s, cache_k, cache_v, i)
            rp = _layer_ab_post(yp, gp, grp_p, mixed_p, wts, zero_hist, 0)
            rs = _layer_ab_post(ys, gs, grp_s, mixed_s, wts, state_pool[i], past_len)
            yp, ys = rp[0], rs[0]
            outs_p.append(rp[1:])
            outs_s.append(rs[1:])
        else:
            wts = {
                "win": w_in_c, "wo": w_out_c, "idx": i,
                "gv": g_v[i], "bv": b_v[i],
                "ws": w_s[i], "bst": b_s[i].T,
            }
            yp, _ = _layer_c(yp, shp, scp, gp, g_norm[layer], wts, False)
            ys, v_c = _layer_c(ys, shs, scs, gs, g_norm[layer], wts, True)
            sgu_v.append(v_c)

    def stack(group, idx):
        return jnp.stack([o[idx] for o in group])

    return (yp, ys,
            stack(outs_p, 0), stack(outs_p, 1), stack(outs_p, 2), stack(outs_p, 3),
            stack(outs_s, 0), stack(outs_s, 1), stack(outs_s, 2), stack(outs_s, 3),
            jnp.stack(sgu_v))
```

```python
import functools

import jax
import jax.numpy as jnp
from jax import lax
from jax.experimental import pallas as pl
from jax.experimental.pallas import tpu as pltpu

F32 = jnp.float32
BF16 = jnp.bfloat16

EPS = 1e-6
HEAD_DIM = 128
POOL_WINDOWS = (2, 4, 8, 16)
POOL_HIST = max(POOL_WINDOWS) - 1
HALO = POOL_HIST + 1
SGU_CHUNK = 128
N_SGU_GROUPS = 16
NEG = -1e30
LOG2E = 1.4426950408889634
LANE = 128
SUBLANE = 8
ROWS_ELEMENTWISE = 512
MIB = 1024 * 1024


def _params(sem, vmem_mib):
    return pltpu.CompilerParams(dimension_semantics=sem, vmem_limit_bytes=vmem_mib * MIB)


def _dot(a, b):
    return jnp.dot(a, b, preferred_element_type=F32)


def _dot_nt(a, b):
    return lax.dot_general(a, b, (((1,), (1,)), ((), ())), preferred_element_type=F32)


def _ada_body(c_ref, w_ref, b_ref, o_ref):
    a = jax.nn.silu(c_ref[...]).astype(BF16)
    o_ref[0] = _dot(a, w_ref[0].astype(BF16)) + b_ref[0]


def _ada_call(c_all, w_ada, b_ada, tn=512):
    depth, d, n = w_ada.shape
    rp = c_all.shape[0]
    return pl.pallas_call(
        _ada_body,
        grid=(depth, n // tn),
        in_specs=[pl.BlockSpec((rp, d), lambda l, j: (0, 0)),
                  pl.BlockSpec((1, d, tn), lambda l, j: (l, 0, j)),
                  pl.BlockSpec((1, 1, tn), lambda l, j: (l, 0, j))],
        out_specs=pl.BlockSpec((1, rp, tn), lambda l, j: (l, 0, j)),
        out_shape=jax.ShapeDtypeStruct((depth, rp, n), F32),
        compiler_params=_params(("arbitrary", "arbitrary"), 40),
        name="ada_mod",
    )(c_all, w_ada, b_ada.reshape(depth, 1, n))


def _normmod_body(x_ref, g_ref, sc_ref, sh_ref, o_ref):
    x = x_ref[...]
    y = x * lax.rsqrt(jnp.mean(x * x, axis=-1, keepdims=True) + EPS) * g_ref[...]
    o_ref[...] = (y * (1 + sc_ref[...]) + sh_ref[...]).astype(BF16)


def _normmod_call(x, g, scale, shift, bb, lb):
    b, l, d = x.shape
    row = pl.BlockSpec((bb, lb, d), lambda i, r: (i, r, 0))
    per_b = pl.BlockSpec((bb, 1, d), lambda i, r: (i, 0, 0))
    return pl.pallas_call(
        _normmod_body,
        grid=(b // bb, l // lb),
        in_specs=[row, pl.BlockSpec((1, 1, d), lambda i, r: (0, 0, 0)), per_b, per_b],
        out_specs=row,
        out_shape=jax.ShapeDtypeStruct((b, l, d), BF16),
        compiler_params=_params(("arbitrary", "arbitrary"), 40),
        name="norm_mod",
    )(x, g.reshape(1, 1, d), scale, shift)


def _cast_body(x_ref, o_ref):
    o_ref[...] = x_ref[...].astype(o_ref.dtype)


def _cast_call(x, dtype, rows=ROWS_ELEMENTWISE):
    nl, r, c = x.shape
    spec = pl.BlockSpec((1, rows, c), lambda l, i: (l, i, 0))
    return pl.pallas_call(
        _cast_body,
        grid=(nl, pl.cdiv(r, rows)),
        in_specs=[spec],
        out_specs=spec,
        out_shape=jax.ShapeDtypeStruct(x.shape, dtype),
        compiler_params=_params(("arbitrary", "arbitrary"), 40),
        name="cast_w",
    )(x)


def _proj_body(a_ref, w_ref, *rest, mode, n_gelu_tiles, w_transposed):
    w = w_ref[...].astype(BF16)
    acc = _dot_nt(a_ref[...], w) if w_transposed else _dot(a_ref[...], w)
    tn = acc.shape[1]
    if mode == "headnorm":
        g_ref, outs = rest[0], rest[1:]
        for hh in range(tn // HEAD_DIM):
            cs = slice(hh * HEAD_DIM, (hh + 1) * HEAD_DIM)
            blk = acc[:, cs]
            y = blk * lax.rsqrt(jnp.mean(blk * blk, axis=-1, keepdims=True) + EPS) * g_ref[...]
            for o_ref in outs:
                o_ref[:, cs] = y.astype(o_ref.dtype)
    elif mode == "plain":
        for o_ref in rest:
            o_ref[...] = acc.astype(o_ref.dtype)
    elif mode == "logsigmoid":
        b_ref, o_ref = rest
        x = acc + b_ref[...]
        o_ref[...] = jnp.minimum(x, 0.0) - jnp.log1p(jnp.exp(-jnp.abs(x)))
    elif mode == "silu":
        (o_ref,) = rest
        o_ref[...] = jax.nn.silu(acc).astype(o_ref.dtype)
    elif mode == "gelu_silu":
        (o_ref,) = rest
        j = pl.program_id(1)

        @pl.when(j < n_gelu_tiles)
        def _():
            o_ref[...] = (0.5 * acc * (1.0 + lax.erf(acc * (2.0 ** -0.5)))).astype(o_ref.dtype)

        @pl.when(j >= n_gelu_tiles)
        def _():
            o_ref[...] = jax.nn.silu(acc).astype(o_ref.dtype)
    else:
        raise ValueError(mode)


def _proj_call(a, w, mode, out_dtypes, extra=None, n_gelu_tiles=0, tm=1024, tn=512, layer=0, cols=None,
               w_transposed=False, name="proj"):
    m, k = a.shape
    col0, n = cols if cols is not None else (0, w.shape[1 if w_transposed else 2])
    tn = min(tn, n)
    assert n % tn == 0
    if w_transposed:
        sub = SUBLANE * 4 // w.dtype.itemsize
        assert col0 % sub == 0 and tn % sub == 0
        wspec = pl.BlockSpec((None, pl.Element(tn), pl.Element(k)),
                             lambda i, j: (layer, (col0 // sub + j * (tn // sub)) * sub, 0))
    else:
        j0 = col0 // tn
        assert j0 * tn == col0
        wspec = pl.BlockSpec((None, k, tn), lambda i, j: (layer, 0, j + j0))
    in_specs = [pl.BlockSpec((tm, k), lambda i, j: (i, 0)), wspec]
    args = [a, w]
    if extra is not None:
        ew = extra.shape[1]
        if ew == n:
            in_specs.append(pl.BlockSpec((1, tn), lambda i, j: (0, j)))
        else:
            in_specs.append(pl.BlockSpec((1, ew), lambda i, j: (0, 0)))
        args.append(extra)
    out_spec = pl.BlockSpec((tm, tn), lambda i, j: (i, j))
    outs = pl.pallas_call(
        functools.partial(_proj_body, mode=mode, n_gelu_tiles=n_gelu_tiles, w_transposed=w_transposed),
        grid=(m // tm, n // tn),
        in_specs=in_specs,
        out_specs=[out_spec] * len(out_dtypes),
        out_shape=[jax.ShapeDtypeStruct((m, n), dt) for dt in out_dtypes],
        compiler_params=_params(("arbitrary", "arbitrary"), 48),
        name=name,
    )(*args)
    return outs


def _outproj_body(*refs, n_a):
    a_refs, w_refs = refs[:n_a], refs[n_a:2 * n_a]
    x_ref, g_ref, o_ref = refs[2 * n_a:]
    bb, lb, tn = x_ref.shape
    acc = None
    for a_ref, w_ref in zip(a_refs, w_refs):
        a = a_ref[...].reshape(bb * lb, a_ref.shape[2])
        d = _dot(a, w_ref[...].astype(BF16))
        acc = d if acc is None else acc + d
    o_ref[...] = x_ref[...] + g_ref[...] * acc.reshape(bb, lb, tn)


def _outproj_call(a_list, w, layer, x, gate, bb, lb, tn=512):
    b, l, n = x.shape
    n_a = len(a_list)
    in_specs = []
    for a in a_list:
        in_specs.append(pl.BlockSpec((bb, lb, a.shape[2]), lambda i, r, j: (i, r, 0)))
    for ai, a in enumerate(a_list):
        in_specs.append(pl.BlockSpec((None, a.shape[2], tn), lambda i, r, j, ai=ai: (layer, ai, j)))
    xspec = pl.BlockSpec((bb, lb, tn), lambda i, r, j: (i, r, j))
    in_specs += [xspec, pl.BlockSpec((bb, 1, tn), lambda i, r, j: (i, 0, j))]
    return pl.pallas_call(
        functools.partial(_outproj_body, n_a=n_a),
        grid=(b // bb, l // lb, n // tn),
        in_specs=in_specs,
        out_specs=xspec,
        out_shape=jax.ShapeDtypeStruct((b, l, n), F32),
        compiler_params=_params(("arbitrary", "arbitrary", "arbitrary"), 48),
        name="out_proj",
    )(*a_list, *([w] * n_a), x, gate)


def _cumsum_body(x_ref, o_ref, *, ch, reverse_exclusive):
    rb, n = x_ref.shape
    ii = lax.broadcasted_iota(jnp.int32, (ch, ch), 0)
    jj = lax.broadcasted_iota(jnp.int32, (ch, ch), 1)
    tri = (ii > jj) if reverse_exclusive else (ii <= jj)
    tri = tri.astype(F32)
    chunks = range(n // ch)
    carry = jnp.zeros((rb, 1), F32)
    for c in (reversed(chunks) if reverse_exclusive else chunks):
        xc = x_ref[:, c * ch:(c + 1) * ch]
        y = jnp.dot(xc, tri, precision=lax.Precision.HIGHEST, preferred_element_type=F32)
        o_ref[:, c * ch:(c + 1) * ch] = y + carry
        carry = carry + jnp.sum(xc, axis=-1, keepdims=True)


def _cumsum_call(x, reverse_exclusive, rb):
    r, n = x.shape
    ch = min(n, 512)
    return pl.pallas_call(
        functools.partial(_cumsum_body, ch=ch, reverse_exclusive=reverse_exclusive),
        grid=(r // rb,),
        in_specs=[pl.BlockSpec((rb, n), lambda i: (i, 0))],
        out_specs=pl.BlockSpec((rb, n), lambda i: (i, 0)),
        out_shape=jax.ShapeDtypeStruct((r, n), F32),
        compiler_params=_params(("arbitrary",), 32),
        name="cumsum",
    )(x)


def _store_scores(raw, h, c1, qb_ref, kbias, mask, s_ref, pm_ref):
    tk = raw.shape[1]
    w = min(tk, LANE)
    pm = None
    for c in range(tk // w):
        cs = slice(c * w, (c + 1) * w)
        sc = raw[:, cs] * c1 + (qb_ref[h, :, :w] + kbias[:, cs])
        if mask is not None:
            sc = jnp.where(mask[:, cs], sc, NEG)
        s_ref[h, :, cs] = sc
        pm = sc if pm is None else jnp.maximum(pm, sc)
    pm_ref[h] = pm


def _softmax_update(s_ref, p_ref, pm_ref, m_ref, a_ref):
    nslots, _, tk = s_ref.shape
    w = pm_ref.shape[2]
    for h in range(nslots):
        m_prev = m_ref[h]
        m_new = jnp.maximum(m_prev, jnp.max(pm_ref[h], axis=-1, keepdims=True))
        a_ref[h] = jnp.exp2(m_prev - m_new)
        m_ref[h] = m_new
    for h in range(nslots):
        for c in range(tk // w):
            cs = slice(c * w, (c + 1) * w)
            p_ref[h, :, cs] = jnp.exp2(s_ref[h, :, cs] - m_ref[h, :, :w]).astype(BF16)


def _accumulate(h, p, v, a_ref, l_ref, acc_ref):
    res = _dot(p, jnp.concatenate([v, jnp.ones_like(v)], axis=1))
    a = a_ref[h]
    return a * acc_ref[h] + res[:, :HEAD_DIM], a * l_ref[h] + res[:, HEAD_DIM:]


def _causal_mask(rows, cols):
    rr = lax.broadcasted_iota(jnp.int32, (rows, cols), 0)
    cc = lax.broadcasted_iota(jnp.int32, (rows, cols), 1)
    return cc <= rr


def _init_softmax_state(m_ref, l_ref, acc_ref):
    m_ref[...] = jnp.full(m_ref.shape, NEG, F32)
    l_ref[...] = jnp.zeros(l_ref.shape, F32)
    acc_ref[...] = jnp.zeros(acc_ref.shape, F32)


def _head_cols(h):
    return slice(h * HEAD_DIM, (h + 1) * HEAD_DIM)


def _prompt_iter(tbl_ref, t, q_ref, k_ref, v_ref, fcol_ref, frow_ref, sg_ref, o_ref,
                 s_ref, p_ref, pm_ref, m_ref, l_ref, a_ref, acc_ref, qb_ref, *, tq, hg, c1):
    g, qi, kj = tbl_ref[1, t], tbl_ref[2, t], tbl_ref[3, t]

    @pl.when(kj == 0)
    def _():
        fblk = fcol_ref[0] * LOG2E
        lane = lax.broadcasted_iota(jnp.int32, fblk.shape, 1)
        for hh in range(hg):
            fc = jnp.sum(jnp.where(lane == g * hg + hh, fblk, 0.0), axis=-1, keepdims=True)
            qb_ref[hh] = jnp.broadcast_to(fc, qb_ref.shape[1:])
        _init_softmax_state(m_ref, l_ref, acc_ref)

    def step(masked):
        rows = pl.ds(pl.multiple_of(kj * tq, tq), tq)
        mask = _causal_mask(tq, tq) if masked else None
        for hh in range(hg):
            raw = _dot_nt(q_ref[0, :, _head_cols(hh)], k_ref[0, rows, _head_cols(hh)])
            kbias = frow_ref[0, kj, hh:hh + 1, :] * -LOG2E
            _store_scores(raw, hh, c1, qb_ref, kbias, mask, s_ref, pm_ref)
        _softmax_update(s_ref, p_ref, pm_ref, m_ref, a_ref)
        for hh in range(hg):
            acc_ref[hh], l_ref[hh] = _accumulate(hh, p_ref[hh], v_ref[0, rows, _head_cols(hh)],
                                                 a_ref, l_ref, acc_ref)

    @pl.when(kj < qi)
    def _():
        step(False)

    @pl.when(kj == qi)
    def _():
        step(True)
        for hh in range(hg):
            cs = _head_cols(hh)
            o_ref[0, :, cs] = (acc_ref[hh] / l_ref[hh] * sg_ref[0, :, cs]).astype(BF16)


def _cache_copy(hbm_ref, buf_ref, sem_ref, layer, bi, pi, slot, h):
    tp = buf_ref.shape[2]
    return pltpu.make_async_copy(hbm_ref.at[layer, bi, pl.ds(pi * tp, tp), h, :],
                                 buf_ref.at[slot, h], sem_ref.at[slot, h])


def _sample_step(t, q_ref, kn_ref, vn_ref, ck_hbm, cv_hbm, fcol_ref, frow_ref, gp_ref, sg_ref, o_ref,
                 kbuf, vbuf, ksem, vsem,
                 s_ref, p_ref, pm_ref, sn_ref, pn_ref, pmn_ref, m_ref, l_ref, a_ref, acc_ref,
                 qb_ref, *, n_p, n_steps, nh, c1, layer):
    bi = lax.div(t, n_p)
    pi = lax.rem(t, n_p)
    slot = lax.rem(t, 2)
    l = q_ref.shape[1]

    def fetch(b_to, p_to, slot_to):
        for h in range(nh):
            _cache_copy(ck_hbm, kbuf, ksem, layer, b_to, p_to, slot_to, h).start()
            _cache_copy(cv_hbm, vbuf, vsem, layer, b_to, p_to, slot_to, h).start()

    @pl.when(t == 0)
    def _():
        fetch(0, 0, 0)

    @pl.when(t + 1 < n_steps)
    def _():
        fetch(lax.div(t + 1, n_p), lax.rem(t + 1, n_p), 1 - slot)

    @pl.when(pi == 0)
    def _():
        _init_softmax_state(m_ref, l_ref, acc_ref)
        for h in range(nh):
            qb_ref[h] = jnp.broadcast_to(fcol_ref[0, :, h:h + 1] * LOG2E, qb_ref.shape[1:])

    for h in range(nh):
        _cache_copy(ck_hbm, kbuf, ksem, layer, bi, pi, slot, h).wait()
        _cache_copy(cv_hbm, vbuf, vsem, layer, bi, pi, slot, h).wait()
    for h in range(nh):
        raw = _dot_nt(q_ref[0, :, _head_cols(h)], kbuf[slot, h].astype(BF16))
        _store_scores(raw, h, c1, qb_ref, gp_ref[0, h:h + 1, :] * LOG2E, None, s_ref, pm_ref)
    _softmax_update(s_ref, p_ref, pm_ref, m_ref, a_ref)
    for h in range(nh):
        acc_ref[h], l_ref[h] = _accumulate(h, p_ref[h], vbuf[slot, h].astype(BF16), a_ref, l_ref, acc_ref)

    @pl.when(pi == n_p - 1)
    def _():
        mask = _causal_mask(l, l)
        for h in range(nh):
            cs = _head_cols(h)
            raw = _dot_nt(q_ref[0, :, cs], kn_ref[0, :, cs])
            _store_scores(raw, h, c1, qb_ref, frow_ref[0, h:h + 1, :] * -LOG2E, mask, sn_ref, pmn_ref)
        _softmax_update(sn_ref, pn_ref, pmn_ref, m_ref, a_ref)
        for h in range(nh):
            cs = _head_cols(h)
            acc, den = _accumulate(h, pn_ref[h], vn_ref[0, :, cs], a_ref, l_ref, acc_ref)
            o_ref[0, :, cs] = (acc / den * sg_ref[0, :, cs]).astype(BF16)


N_PROMPT_IN, N_SAMPLE_IN, N_PROMPT_SCRATCH = 6, 9, 8


def _attn_body(tbl_ref, *refs, n_prompt, n_sample, n_p, tq, hg, nh, c1, layer):
    t = pl.program_id(0)
    p_in = refs[:N_PROMPT_IN]
    s_in = refs[N_PROMPT_IN:N_PROMPT_IN + N_SAMPLE_IN]
    op_ref, os_ref = refs[N_PROMPT_IN + N_SAMPLE_IN:N_PROMPT_IN + N_SAMPLE_IN + 2]
    scratch = refs[N_PROMPT_IN + N_SAMPLE_IN + 2:]
    p_scr, s_scr = scratch[:N_PROMPT_SCRATCH], scratch[N_PROMPT_SCRATCH:]

    @pl.when(t < n_sample)
    def _():
        _sample_step(t, *s_in, os_ref, *s_scr, n_p=n_p, n_steps=n_sample, nh=nh, c1=c1, layer=layer)

    @pl.when(t < n_prompt)
    def _():
        _prompt_iter(tbl_ref, t, *p_in, op_ref, *p_scr, tq=tq, hg=hg, c1=c1)


def _attn_call(pr, sm, ck, cv, layer, tq=512, hg=4, tp=512):
    bp, s, aw = pr["q"].shape
    bs, l, _ = sm["q"].shape
    nh = aw // HEAD_DIM
    ng, nq, gw = nh // hg, s // tq, hg * HEAD_DIM
    n_p = ck.shape[2] // tp
    sched = [(b, g, qi, kj) for b in range(bp) for g in range(ng) for qi in range(nq) for kj in range(qi + 1)]
    n_prompt, n_sample = len(sched), bs * n_p
    n_steps = max(n_prompt, n_sample)
    sched += [sched[-1]] * (n_steps - n_prompt)
    tbl = jnp.asarray(sched, jnp.int32).T

    qspec = pl.BlockSpec((1, tq, gw), lambda t, tb: (tb[0, t], tb[2, t], tb[1, t]))
    kvspec = pl.BlockSpec((1, s, gw), lambda t, tb: (tb[0, t], 0, tb[1, t]), pipeline_mode=pl.Buffered(1))
    frow4 = pr["frow"].reshape(bp * ng, hg, nq, tq).transpose(0, 2, 1, 3)
    prompt_specs = [qspec, kvspec, kvspec,
                    pl.BlockSpec((1, tq, nh), lambda t, tb: (tb[0, t], tb[2, t], 0)),
                    pl.BlockSpec((1, nq, hg, tq), lambda t, tb: (tb[0, t] * ng + tb[1, t], 0, 0, 0)),
                    qspec]

    def sb(t):
        return jnp.minimum(t // n_p, bs - 1)

    new = pl.BlockSpec((1, l, aw), lambda t, tb: (sb(t), 0, 0))
    cache = pl.BlockSpec(memory_space=pl.ANY)
    sample_specs = [new, new, new, cache, cache,
                    pl.BlockSpec((1, l, nh), lambda t, tb: (sb(t), 0, 0)),
                    pl.BlockSpec((1, nh, l), lambda t, tb: (sb(t), 0, 0)),
                    pl.BlockSpec((1, nh, tp), lambda t, tb: (sb(t), 0, jnp.where(t < n_sample, t % n_p, n_p - 1))),
                    new]

    prep = pltpu.VMEM((hg, tq, LANE), F32)
    prompt_scratch = [pltpu.VMEM((hg, tq, tq), F32), pltpu.VMEM((hg, tq, tq), BF16)] + [prep] * 6
    cbuf = pltpu.VMEM((2, nh, tp, HEAD_DIM), ck.dtype)
    csem = pltpu.SemaphoreType.DMA((2, nh))
    srep = pltpu.VMEM((nh, l, LANE), F32)
    sample_scratch = [cbuf, cbuf, csem, csem,
                      pltpu.VMEM((nh, l, tp), F32), pltpu.VMEM((nh, l, tp), BF16), srep,
                      pltpu.VMEM((nh, l, l), F32), pltpu.VMEM((nh, l, l), BF16),
                      pltpu.VMEM((nh, l, min(l, LANE)), F32)] + [srep] * 5
    assert len(prompt_specs) == N_PROMPT_IN and len(sample_specs) == N_SAMPLE_IN
    assert len(prompt_scratch) == N_PROMPT_SCRATCH

    return pl.pallas_call(
        functools.partial(_attn_body, n_prompt=n_prompt, n_sample=n_sample, n_p=n_p, tq=tq, hg=hg, nh=nh,
                          c1=HEAD_DIM ** -0.5 * LOG2E, layer=layer),
        grid_spec=pltpu.PrefetchScalarGridSpec(
            num_scalar_prefetch=1, grid=(n_steps,),
            in_specs=prompt_specs + sample_specs,
            out_specs=[qspec, new],
            scratch_shapes=prompt_scratch + sample_scratch),
        out_shape=[jax.ShapeDtypeStruct((bp, s, aw), BF16), jax.ShapeDtypeStruct((bs, l, aw), BF16)],
        compiler_params=_params(("arbitrary",), 56),
        name="attn",
    )(tbl, pr["q"], pr["k"], pr["v"], pr["fcol"], frow4, pr["sgate"],
      sm["q"], sm["k"], sm["v"], ck, cv, sm["fcol"], sm["frow"], sm["gpast"], sm["sgate"])


def _pool_body(u_ref, halo_ref, hist_ref, sg_ref, w_ref, ls_ref, o_ref, ext_ref, *, pos0):
    r = pl.program_id(1)
    tm = u_ref.shape[1]
    group = u_ref.shape[2] // len(POOL_WINDOWS)
    ext_ref[HALO:HALO + tm, :] = u_ref[0]

    @pl.when(r == 0)
    def _():
        ext_ref[0:HALO, :] = hist_ref[0]

    @pl.when(r > 0)
    def _():
        ext_ref[0:HALO, :] = halo_ref[0]

    n_before = lax.broadcasted_iota(jnp.int32, (tm, 1), 0) + (pos0 + 1) + r * tm
    for gi, w in enumerate(POOL_WINDOWS):
        cs = slice(gi * group, (gi + 1) * group)
        win = ext_ref[HALO:HALO + tm, cs]
        for i in range(1, w):
            win = win + ext_ref[HALO - i:HALO - i + tm, cs]
        cnt = jnp.minimum(w, n_before).astype(F32)
        d = win / cnt - u_ref[0, :, cs]
        y = _dot(d.astype(BF16), w_ref[gi]) * ls_ref[:, cs]
        o_ref[0, :, cs] = (y * sg_ref[0, :, cs]).astype(BF16)


def _pool_call(u, hist16, sgate, w_pool, ls_pool, pos0, tm):
    b, l, bw = u.shape
    g = w_pool.shape[0]
    row = pl.BlockSpec((1, tm, bw), lambda bi, r: (bi, r, 0))
    halo_blocks = tm // HALO
    return pl.pallas_call(
        functools.partial(_pool_body, pos0=pos0),
        grid=(b, l // tm),
        in_specs=[row,
                  pl.BlockSpec((1, HALO, bw), lambda bi, r: (bi, jnp.maximum(r * halo_blocks - 1, 0), 0)),
                  pl.BlockSpec((1, HALO, bw), lambda bi, r: (bi, 0, 0)),
                  pl.BlockSpec((1, tm, bw), lambda bi, r: (bi, r, 1)),
                  pl.BlockSpec((g, bw // g, bw // g), lambda bi, r: (0, 0, 0)),
                  pl.BlockSpec((1, bw), lambda bi, r: (0, 0))],
        out_specs=row,
        out_shape=jax.ShapeDtypeStruct((b, l, bw), BF16),
        scratch_shapes=[pltpu.VMEM((HALO + tm, bw), F32)],
        compiler_params=_params(("arbitrary", "arbitrary"), 40),
        name="pool_mix",
    )(u, u, hist16, sgate, w_pool, ls_pool.reshape(1, bw))


def _sgu_body(u_ref, v_ref, gt_ref, gv_ref, bv_ref, ws_ref, bst_ref, o_ref, *vout, cl):
    lb, cw = u_ref.shape[1], u_ref.shape[2]
    gw = cw // N_SGU_GROUPS
    rr = lax.broadcasted_iota(jnp.int32, (cl, cl), 0)
    cc = lax.broadcasted_iota(jnp.int32, (cl, cl), 1)
    for c in range(lb // cl):
        rows = slice(c * cl, (c + 1) * cl)
        v = v_ref[0, rows, :].astype(F32)
        xc = v - jnp.mean(v, axis=-1, keepdims=True)
        var = jnp.mean(xc * xc, axis=-1, keepdims=True)
        vln = xc * lax.rsqrt(var + EPS) * gv_ref[...] + bv_ref[...]
        if vout:
            vout[0][0, rows, :] = vln
        for g in range(N_SGU_GROUPS):
            cs = slice(g * gw, (g + 1) * gw)
            ws = jnp.where(cc <= rr, ws_ref[g, :cl, :cl], 0.0).astype(BF16)
            sv = _dot(ws, vln[:, cs].astype(BF16)) + bst_ref[:cl, g:g + 1]
            o_ref[0, rows, cs] = (u_ref[0, rows, cs] * sv * gt_ref[0, rows, cs]).astype(BF16)


def _sgu_call(zact, g_v, b_v, w_s, b_s_t, lb, cl, want_v):
    b, l, cw3 = zact.shape
    cw = cw3 // 3
    out_spec = pl.BlockSpec((1, lb, cw), lambda bi, r: (bi, r, 0))
    out_shape = [jax.ShapeDtypeStruct((b, l, cw), BF16)]
    out_specs = [out_spec]
    if want_v:
        out_shape.append(jax.ShapeDtypeStruct((b, l, cw), F32))
        out_specs.append(out_spec)
    vec = pl.BlockSpec((1, cw), lambda bi, r: (0, 0))
    return pl.pallas_call(
        functools.partial(_sgu_body, cl=cl),
        grid=(b, l // lb),
        in_specs=[pl.BlockSpec((1, lb, cw), lambda bi, r: (bi, r, 0)),
                  pl.BlockSpec((1, lb, cw), lambda bi, r: (bi, r, 1)),
                  pl.BlockSpec((1, lb, cw), lambda bi, r: (bi, r, 2)),
                  vec, vec,
                  pl.BlockSpec(w_s.shape, lambda bi, r: (0, 0, 0)),
                  pl.BlockSpec(b_s_t.shape, lambda bi, r: (0, 0))],
        out_specs=out_specs,
        out_shape=out_shape,
        compiler_params=_params(("arbitrary", "arbitrary"), 48),
        name="sgu",
    )(zact, zact, zact, g_v.reshape(1, cw), b_v.reshape(1, cw), w_s, b_s_t)


def _row_blocking(b, l, rows=1024):
    if l >= rows:
        return 1, rows
    return rows // l, l


def _layer_ab_pre(x, shift, scale, g_norm, wts, clogf):
    b, l, d = x.shape
    col = wts["cols"]
    aw = col["q"][1]
    nh = aw // HEAD_DIM
    h = _normmod_call(x, g_norm, scale, shift, *_row_blocking(b, l, ROWS_ELEMENTWISE)).reshape(b * l, d)

    proj = functools.partial(_proj_call, h, wts["win_t"], layer=wts["idx"], w_transposed=True)
    (qn,) = proj("headnorm", [BF16], extra=wts["gq"], cols=col["q"], name="proj_q")
    k32, k16 = proj("headnorm", [F32, BF16], extra=wts["gk"], cols=col["k"], name="proj_k")
    v32, v16 = proj("plain", [F32, BF16], cols=col["v"], name="proj_v")
    (logf,) = proj("logsigmoid", [F32], extra=wts["bf"], cols=col["f"], name="proj_f")
    (u,) = proj("plain", [F32], cols=col["u"], name="proj_u")
    (sgate,) = proj("silu", [BF16], cols=col["g"], name="proj_gate")

    logf = logf.reshape(b, l, nh)
    logf_t = logf.transpose(0, 2, 1).reshape(b * nh, l)
    frow = _cumsum_call(logf_t, False, min(b * nh, 64)).reshape(b, nh, l)
    grp = {"q": qn.reshape(b, l, aw), "k": k16.reshape(b, l, aw), "v": v16.reshape(b, l, aw),
           "frow": frow, "fcol": frow.transpose(0, 2, 1), "sgate": sgate.reshape(b, l, -1),
           "u": u.reshape(b, l, -1), "logf": logf,
           "k32": k32.reshape(b, l, nh, HEAD_DIM), "v32": v32.reshape(b, l, nh, HEAD_DIM)}
    if clogf is not None:
        p = clogf.shape[1]
        clf_t = clogf.transpose(0, 2, 1).reshape(b * nh, p)
        grp["gpast"] = _cumsum_call(clf_t, True, 64).reshape(b, nh, p)
    return grp


def _layer_ab_post(x, gate, grp, mixed_a, wts, hist, pos0):
    b, l, _ = x.shape
    bb, lb = _row_blocking(b, l)
    u3 = grp["u"]
    hist16 = jnp.pad(hist, ((0, 0), (HALO - POOL_HIST, 0), (0, 0)))
    mixed_b = _pool_call(u3, hist16, grp["sgate"], wts["wpool"], wts["lspool"], pos0, min(l, ROWS_ELEMENTWISE))
    y = _outproj_call([mixed_a, mixed_b], wts["wo"], wts["idx"], x, gate, bb, lb)
    if l >= POOL_HIST:
        new_hist = u3[:, l - POOL_HIST:]
    else:
        new_hist = jnp.concatenate([hist, u3], axis=1)[:, -POOL_HIST:]
    return y, grp["k32"], grp["v32"], grp["logf"], new_hist


def _layer_c(x, shift, scale, gate, g_norm, wts, want_v):
    b, l, d = x.shape
    bb, lb = _row_blocking(b, l)
    h = _normmod_call(x, g_norm, scale, shift, *_row_blocking(b, l, ROWS_ELEMENTWISE)).reshape(b * l, d)
    cw = wts["wo"].shape[1]
    tn = 512
    (zact,) = _proj_call(h, wts["win"], "gelu_silu", [BF16], n_gelu_tiles=2 * cw // tn, tn=tn,
                         layer=wts["idx"], name="proj_c")
    cl = min(l, SGU_CHUNK)
    outs = _sgu_call(zact.reshape(b, l, 3 * cw), wts["gv"], wts["bv"], wts["ws"], wts["bst"],
                     min(l, ROWS_ELEMENTWISE), cl, want_v)
    y = _outproj_call([outs[0]], wts["wo"], wts["idx"], x, gate, bb, lb)
    return y, (outs[1] if want_v else None)


def kernel(x_prompt, x_sample, cache_k, cache_v, cache_logf, state_pool, c_prompt, c_sample,
           w_ada, b_ada, g_norm, w_in_ab, b_forget, g_q, g_k, w_pool, ls_pool, w_out_ab,
           w_in_c, g_v, b_v, w_s, b_s, w_out_c):
    bp, sp, d = x_prompt.shape
    bs = x_sample.shape[0]
    depth = w_ada.shape[0]
    nh = cache_k.shape[3]
    aw = nh * HEAD_DIM
    bw = w_pool.shape[2] * w_pool.shape[1]
    past_len = cache_k.shape[2]

    c_all = jnp.concatenate([c_prompt, c_sample], axis=0)
    c_all = jnp.pad(c_all, ((0, -c_all.shape[0] % 8), (0, 0)))
    mod = _ada_call(c_all, w_ada, b_ada)

    def mods(layer, lo, n):
        m = mod[layer, lo:lo + n].reshape(n, 1, 3 * d)
        return m[..., :d], m[..., d:2 * d], m[..., 2 * d:]

    yp, ys = x_prompt, x_sample
    outs_p, outs_s, sgu_v = [], [], []
    for layer in range(depth):
        i = layer // 2
        shp, scp, gp = mods(layer, 0, bp)
        shs, scs, gs = mods(layer, bp, bs)
        if layer % 2 == 0:
            o_f, o_u, o_g = 3 * aw, 3 * aw + nh, 3 * aw + nh + bw
            wts = {
                "win_t": _cast_call(jnp.swapaxes(w_in_ab, 1, 2), BF16),
                "cols": {"q": (0, aw), "k": (aw, aw), "v": (2 * aw, aw), "f": (o_f, nh), "u": (o_u, bw),
                         "g": (o_g, w_in_ab.shape[2] - o_g)},
                "bf": b_forget[i].reshape(1, nh),
                "gq": g_q[i].reshape(1, HEAD_DIM),
                "gk": g_k[i].reshape(1, HEAD_DIM),
                "wpool": w_pool[i].astype(BF16),
                "lspool": ls_pool[i],
                "wo": w_out_ab, "idx": i,
            }
            zero_hist = jnp.zeros((bp, POOL_HIST, bw), F32)
            grp_p = _layer_ab_pre(yp, shp, scp, g_norm[layer], wts, None)
            grp_s = _layer_ab_pre(ys, shs, scs, g_norm[layer], wts, cache_logf[i])
            mixed_p, mixed_s = _attn_call(grp_p, grp_s, cache_k, cache_v, i)
            rp = _layer_ab_post(yp, gp, grp_p, mixed_p, wts, zero_hist, 0)
            rs = _layer_ab_post(ys, gs, grp_s, mixed_s, wts, state_pool[i], past_len)
            yp, ys = rp[0], rs[0]
            outs_p.append(rp[1:])
            outs_s.append(rs[1:])
        else:
            wts = {
                "win": w_in_c, "wo": w_out_c, "idx": i,
                "gv": g_v[i], "bv": b_v[i],
                "ws": w_s[i], "bst": b_s[i].T,
            }
            yp, _ = _layer_c(yp, shp, scp, gp, g_norm[layer], wts, False)
            ys, v_c = _layer_c(ys, shs, scs, gs, g_norm[layer], wts, True)
            sgu_v.append(v_c)

    def stack(group, idx):
        return jnp.stack([o[idx] for o in group])

    return (yp, ys,
            stack(outs_p, 0), stack(outs_p, 1), stack(outs_p, 2), stack(outs_p, 3),
            stack(outs_s, 0), stack(outs_s, 1), stack(outs_s, 2), stack(outs_s, 3),
            jnp.stack(sgu_v))
```

```python
import functools

import jax
import jax.numpy as jnp
from jax import lax
from jax.experimental import pallas as pl
from jax.experimental.pallas import tpu as pltpu

F32 = jnp.float32
BF16 = jnp.bfloat16

EPS = 1e-6
HEAD_DIM = 128
POOL_WINDOWS = (2, 4, 8, 16)
POOL_HIST = max(POOL_WINDOWS) - 1
HALO = POOL_HIST + 1
SGU_CHUNK = 128
N_SGU_GROUPS = 16
NEG = -1e30
LOG2E = 1.4426950408889634
LANE = 128
ROWS_ELEMENTWISE = 512
MIB = 1024 * 1024


def _params(sem, vmem_mib):
    return pltpu.CompilerParams(dimension_semantics=sem, vmem_limit_bytes=vmem_mib * MIB)


def _dot(a, b):
    return jnp.dot(a, b, preferred_element_type=F32)


def _dot_nt(a, b):
    return lax.dot_general(a, b, (((1,), (1,)), ((), ())), preferred_element_type=F32)


def _ada_body(c_ref, w_ref, b_ref, o_ref):
    a = jax.nn.silu(c_ref[...]).astype(BF16)
    o_ref[0] = _dot(a, w_ref[0].astype(BF16)) + b_ref[0]


def _ada_call(c_all, w_ada, b_ada, stride, tn=512):
    depth, d, n = w_ada.shape
    rp = c_all.shape[0]
    n_l = -(-depth // stride)
    return pl.pallas_call(
        _ada_body,
        grid=(n_l, n // tn),
        in_specs=[pl.BlockSpec((rp, d), lambda l, j: (0, 0)),
                  pl.BlockSpec((1, d, tn), lambda l, j: (l * stride, 0, j)),
                  pl.BlockSpec((1, 1, tn), lambda l, j: (l * stride, 0, j))],
        out_specs=pl.BlockSpec((1, rp, tn), lambda l, j: (l, 0, j)),
        out_shape=jax.ShapeDtypeStruct((n_l, rp, n), F32),
        compiler_params=_params(("arbitrary", "arbitrary"), 40),
        name="ada_mod",
    )(c_all, w_ada, b_ada.reshape(depth, 1, n))


def _normmod_body(x_ref, g_ref, sc_ref, sh_ref, o_ref):
    x = x_ref[...]
    y = x * lax.rsqrt(jnp.mean(x * x, axis=-1, keepdims=True) + EPS) * g_ref[...]
    o_ref[...] = (y * (1 + sc_ref[...]) + sh_ref[...]).astype(BF16)


def _normmod_call(x, g, scale, shift, bb, lb):
    b, l, d = x.shape
    row = pl.BlockSpec((bb, lb, d), lambda i, r: (i, r, 0))
    per_b = pl.BlockSpec((bb, 1, d), lambda i, r: (i, 0, 0))
    return pl.pallas_call(
        _normmod_body,
        grid=(b // bb, l // lb),
        in_specs=[row, pl.BlockSpec((1, 1, d), lambda i, r: (0, 0, 0)), per_b, per_b],
        out_specs=row,
        out_shape=jax.ShapeDtypeStruct((b, l, d), BF16),
        compiler_params=_params(("arbitrary", "arbitrary"), 40),
        name="norm_mod",
    )(x, g.reshape(1, 1, d), scale, shift)


def _proj_body(a_ref, w_ref, *rest, mode, n_gelu_tiles, w_transposed):
    w = w_ref[...].astype(BF16)
    acc = _dot_nt(a_ref[...], w) if w_transposed else _dot(a_ref[...], w)
    tn = acc.shape[1]
    if mode == "headnorm":
        g_ref, outs = rest[0], rest[1:]
        for hh in range(tn // HEAD_DIM):
            cs = slice(hh * HEAD_DIM, (hh + 1) * HEAD_DIM)
            blk = acc[:, cs]
            y = blk * lax.rsqrt(jnp.mean(blk * blk, axis=-1, keepdims=True) + EPS) * g_ref[...]
            for o_ref in outs:
                o_ref[:, cs] = y.astype(o_ref.dtype)
    elif mode == "plain":
        for o_ref in rest:
            o_ref[...] = acc.astype(o_ref.dtype)
    elif mode == "logsigmoid":
        b_ref, o_ref = rest
        x = acc + b_ref[...]
        o_ref[...] = jnp.minimum(x, 0.0) - jnp.log1p(jnp.exp(-jnp.abs(x)))
    elif mode == "silu":
        (o_ref,) = rest
        o_ref[...] = jax.nn.silu(acc).astype(o_ref.dtype)
    elif mode == "gelu_silu":
        (o_ref,) = rest
        j = pl.program_id(1)

        @pl.when(j < n_gelu_tiles)
        def _():
            o_ref[...] = (0.5 * acc * (1.0 + lax.erf(acc * (2.0 ** -0.5)))).astype(o_ref.dtype)

        @pl.when(j >= n_gelu_tiles)
        def _():
            o_ref[...] = jax.nn.silu(acc).astype(o_ref.dtype)
    else:
        raise ValueError(mode)


def _proj_call(a, w, mode, out_dtypes, extra=None, n_gelu_tiles=0, tm=1024, tn=512, layer=0, cols=None,
               w_transposed=False, name="proj"):
    m, k = a.shape
    col0, n = cols if cols is not None else (0, w.shape[1 if w_transposed else 2])
    tn = min(tn, n)
    assert n % tn == 0
    if w_transposed:
        sub = 8
        assert col0 % sub == 0 and tn % sub == 0
        wspec = pl.BlockSpec((None, pl.Element(tn), pl.Element(k)),
                             lambda i, j: (layer, (col0 // sub + j * (tn // sub)) * sub, 0))
    else:
        j0 = col0 // tn
        assert j0 * tn == col0
        wspec = pl.BlockSpec((None, k, tn), lambda i, j: (layer, 0, j + j0))
    in_specs = [pl.BlockSpec((tm, k), lambda i, j: (i, 0)), wspec]
    args = [a, w]
    if extra is not None:
        ew = extra.shape[1]
        if ew == n:
            in_specs.append(pl.BlockSpec((1, tn), lambda i, j: (0, j)))
        else:
            in_specs.append(pl.BlockSpec((1, ew), lambda i, j: (0, 0)))
        args.append(extra)
    out_spec = pl.BlockSpec((tm, tn), lambda i, j: (i, j))
    outs = pl.pallas_call(
        functools.partial(_proj_body, mode=mode, n_gelu_tiles=n_gelu_tiles, w_transposed=w_transposed),
        grid=(m // tm, n // tn),
        in_specs=in_specs,
        out_specs=[out_spec] * len(out_dtypes),
        out_shape=[jax.ShapeDtypeStruct((m, n), dt) for dt in out_dtypes],
        compiler_params=_params(("arbitrary", "arbitrary"), 48),
        name=name,
    )(*args)
    return outs


def _outproj_body(*refs, n_a):
    a_refs, w_refs = refs[:n_a], refs[n_a:2 * n_a]
    x_ref, g_ref, o_ref = refs[2 * n_a:]
    bb, lb, tn = x_ref.shape
    acc = None
    for a_ref, w_ref in zip(a_refs, w_refs):
        a = a_ref[...].reshape(bb * lb, a_ref.shape[2])
        d = _dot(a, w_ref[...].astype(BF16))
        acc = d if acc is None else acc + d
    o_ref[...] = x_ref[...] + g_ref[...] * acc.reshape(bb, lb, tn)


def _outproj_call(a_list, w, layer, x, gate, bb, lb, tn=512):
    b, l, n = x.shape
    n_a = len(a_list)
    in_specs = []
    for a in a_list:
        in_specs.append(pl.BlockSpec((bb, lb, a.shape[2]), lambda i, r, j: (i, r, 0)))
    for ai, a in enumerate(a_list):
        in_specs.append(pl.BlockSpec((None, a.shape[2], tn), lambda i, r, j, ai=ai: (layer, ai, j)))
    xspec = pl.BlockSpec((bb, lb, tn), lambda i, r, j: (i, r, j))
    in_specs += [xspec, pl.BlockSpec((bb, 1, tn), lambda i, r, j: (i, 0, j))]
    return pl.pallas_call(
        functools.partial(_outproj_body, n_a=n_a),
        grid=(b // bb, l // lb, n // tn),
        in_specs=in_specs,
        out_specs=xspec,
        out_shape=jax.ShapeDtypeStruct((b, l, n), F32),
        compiler_params=_params(("arbitrary", "arbitrary", "arbitrary"), 48),
        name="out_proj",
    )(*a_list, *([w] * n_a), x, gate)


def _cumsum_body(x_ref, o_ref, *, ch, reverse_exclusive):
    rb, n = x_ref.shape
    ii = lax.broadcasted_iota(jnp.int32, (ch, ch), 0)
    jj = lax.broadcasted_iota(jnp.int32, (ch, ch), 1)
    tri = (ii > jj) if reverse_exclusive else (ii <= jj)
    tri = tri.astype(F32)
    chunks = range(n // ch)
    carry = jnp.zeros((rb, 1), F32)
    for c in (reversed(chunks) if reverse_exclusive else chunks):
        xc = x_ref[:, c * ch:(c + 1) * ch]
        y = jnp.dot(xc, tri, precision=lax.Precision.HIGHEST, preferred_element_type=F32)
        o_ref[:, c * ch:(c + 1) * ch] = y + carry
        carry = carry + jnp.sum(xc, axis=-1, keepdims=True)


def _cumsum_call(x, reverse_exclusive, rb):
    r, n = x.shape
    ch = min(n, 512)
    return pl.pallas_call(
        functools.partial(_cumsum_body, ch=ch, reverse_exclusive=reverse_exclusive),
        grid=(r // rb,),
        in_specs=[pl.BlockSpec((rb, n), lambda i: (i, 0))],
        out_specs=pl.BlockSpec((rb, n), lambda i: (i, 0)),
        out_shape=jax.ShapeDtypeStruct((r, n), F32),
        compiler_params=_params(("arbitrary",), 32),
        name="cumsum",
    )(x)


def _store_scores(raw, h, c1, qb_ref, kbias, mask, s_ref, pm_ref):
    tk = raw.shape[1]
    w = min(tk, LANE)
    pm = None
    for c in range(tk // w):
        cs = slice(c * w, (c + 1) * w)
        sc = raw[:, cs] * c1 + (qb_ref[h, :, :w] + kbias[:, cs])
        if mask is not None:
            sc = jnp.where(mask[:, cs], sc, NEG)
        s_ref[h, :, cs] = sc
        pm = sc if pm is None else jnp.maximum(pm, sc)
    pm_ref[h] = pm


def _softmax_update(s_ref, p_ref, pm_ref, m_ref, a_ref):
    nslots, _, tk = s_ref.shape
    w = pm_ref.shape[2]
    for h in range(nslots):
        m_prev = m_ref[h]
        m_new = jnp.maximum(m_prev, jnp.max(pm_ref[h], axis=-1, keepdims=True))
        a_ref[h] = jnp.exp2(m_prev - m_new)
        m_ref[h] = m_new
    for h in range(nslots):
        for c in range(tk // w):
            cs = slice(c * w, (c + 1) * w)
            p_ref[h, :, cs] = jnp.exp2(s_ref[h, :, cs] - m_ref[h, :, :w]).astype(BF16)


def _accumulate(h, p, v, a_ref, l_ref, acc_ref):
    res = _dot(p, jnp.concatenate([v, jnp.ones_like(v)], axis=1))
    a = a_ref[h]
    return a * acc_ref[h] + res[:, :HEAD_DIM], a * l_ref[h] + res[:, HEAD_DIM:]


def _causal_mask(rows, cols):
    rr = lax.broadcasted_iota(jnp.int32, (rows, cols), 0)
    cc = lax.broadcasted_iota(jnp.int32, (rows, cols), 1)
    return cc <= rr


def _init_softmax_state(m_ref, l_ref, acc_ref):
    m_ref[...] = jnp.full(m_ref.shape, NEG, F32)
    l_ref[...] = jnp.zeros(l_ref.shape, F32)
    acc_ref[...] = jnp.zeros(acc_ref.shape, F32)


def _head_cols(h):
    return slice(h * HEAD_DIM, (h + 1) * HEAD_DIM)


def _prompt_iter(tbl_ref, t, q_ref, k_ref, v_ref, fcol_ref, frow_ref, sg_ref, o_ref,
                 s_ref, p_ref, pm_ref, m_ref, l_ref, a_ref, acc_ref, qb_ref, *, tq, hg, c1):
    g, qi, kj = tbl_ref[1, t], tbl_ref[2, t], tbl_ref[3, t]

    @pl.when(kj == 0)
    def _():
        fblk = fcol_ref[0] * LOG2E
        lane = lax.broadcasted_iota(jnp.int32, fblk.shape, 1)
        for hh in range(hg):
            fc = jnp.sum(jnp.where(lane == g * hg + hh, fblk, 0.0), axis=-1, keepdims=True)
            qb_ref[hh] = jnp.broadcast_to(fc, qb_ref.shape[1:])
        _init_softmax_state(m_ref, l_ref, acc_ref)

    def step(masked):
        rows = pl.ds(pl.multiple_of(kj * tq, tq), tq)
        mask = _causal_mask(tq, tq) if masked else None
        for hh in range(hg):
            raw = _dot_nt(q_ref[0, :, _head_cols(hh)], k_ref[0, rows, _head_cols(hh)])
            kbias = frow_ref[0, kj, hh:hh + 1, :] * -LOG2E
            _store_scores(raw, hh, c1, qb_ref, kbias, mask, s_ref, pm_ref)
        _softmax_update(s_ref, p_ref, pm_ref, m_ref, a_ref)
        for hh in range(hg):
            acc_ref[hh], l_ref[hh] = _accumulate(hh, p_ref[hh], v_ref[0, rows, _head_cols(hh)],
                                                 a_ref, l_ref, acc_ref)

    @pl.when(kj < qi)
    def _():
        step(False)

    @pl.when(kj == qi)
    def _():
        step(True)
        for hh in range(hg):
            cs = _head_cols(hh)
            o_ref[0, :, cs] = (acc_ref[hh] / l_ref[hh] * sg_ref[0, :, cs]).astype(BF16)


def _cache_copy(hbm_ref, buf_ref, sem_ref, layer, bi, pi, slot, h):
    tp = buf_ref.shape[2]
    return pltpu.make_async_copy(hbm_ref.at[layer, bi, pl.ds(pi * tp, tp), h, :],
                                 buf_ref.at[slot, h], sem_ref.at[slot, h])


def _sample_step(t, q_ref, kn_ref, vn_ref, ck_hbm, cv_hbm, fcol_ref, frow_ref, gp_ref, sg_ref, o_ref,
                 kbuf, vbuf, ksem, vsem,
                 s_ref, p_ref, pm_ref, sn_ref, pn_ref, pmn_ref, m_ref, l_ref, a_ref, acc_ref,
                 qb_ref, *, n_p, n_steps, nh, c1, layer):
    bi = lax.div(t, n_p)
    pi = lax.rem(t, n_p)
    slot = lax.rem(t, 2)
    l = q_ref.shape[1]

    def fetch(b_to, p_to, slot_to):
        for h in range(nh):
            _cache_copy(ck_hbm, kbuf, ksem, layer, b_to, p_to, slot_to, h).start()
            _cache_copy(cv_hbm, vbuf, vsem, layer, b_to, p_to, slot_to, h).start()

    @pl.when(t == 0)
    def _():
        fetch(0, 0, 0)

    @pl.when(t + 1 < n_steps)
    def _():
        fetch(lax.div(t + 1, n_p), lax.rem(t + 1, n_p), 1 - slot)

    @pl.when(pi == 0)
    def _():
        _init_softmax_state(m_ref, l_ref, acc_ref)
        for h in range(nh):
            qb_ref[h] = jnp.broadcast_to(fcol_ref[0, :, h:h + 1] * LOG2E, qb_ref.shape[1:])

    for h in range(nh):
        _cache_copy(ck_hbm, kbuf, ksem, layer, bi, pi, slot, h).wait()
        _cache_copy(cv_hbm, vbuf, vsem, layer, bi, pi, slot, h).wait()
    for h in range(nh):
        raw = _dot_nt(q_ref[0, :, _head_cols(h)], kbuf[slot, h].astype(BF16))
        _store_scores(raw, h, c1, qb_ref, gp_ref[0, h:h + 1, :] * LOG2E, None, s_ref, pm_ref)
    _softmax_update(s_ref, p_ref, pm_ref, m_ref, a_ref)
    for h in range(nh):
        acc_ref[h], l_ref[h] = _accumulate(h, p_ref[h], vbuf[slot, h].astype(BF16), a_ref, l_ref, acc_ref)

    @pl.when(pi == n_p - 1)
    def _():
        mask = _causal_mask(l, l)
        for h in range(nh):
            cs = _head_cols(h)
            raw = _dot_nt(q_ref[0, :, cs], kn_ref[0, :, cs])
            _store_scores(raw, h, c1, qb_ref, frow_ref[0, h:h + 1, :] * -LOG2E, mask, sn_ref, pmn_ref)
        _softmax_update(sn_ref, pn_ref, pmn_ref, m_ref, a_ref)
        for h in range(nh):
            cs = _head_cols(h)
            acc, den = _accumulate(h, pn_ref[h], vn_ref[0, :, cs], a_ref, l_ref, acc_ref)
            o_ref[0, :, cs] = (acc / den * sg_ref[0, :, cs]).astype(BF16)


N_PROMPT_IN, N_SAMPLE_IN, N_ADA_IN, N_PROMPT_SCRATCH = 6, 9, 3, 8


def _attn_body(tbl_ref, *refs, n_prompt, n_sample, n_ada, n_p, tq, hg, nh, c1, layer):
    t = pl.program_id(0)
    n_in = N_PROMPT_IN + N_SAMPLE_IN
    p_in, s_in, a_in = refs[:N_PROMPT_IN], refs[N_PROMPT_IN:n_in], refs[n_in:n_in + N_ADA_IN]
    op_ref, os_ref, oa_ref = refs[n_in + N_ADA_IN:n_in + N_ADA_IN + 3]
    scratch = refs[n_in + N_ADA_IN + 3:]
    p_scr, s_scr = scratch[:N_PROMPT_SCRATCH], scratch[N_PROMPT_SCRATCH:]

    @pl.when(t < n_ada)
    def _():
        _ada_body(*a_in, oa_ref)

    @pl.when(t < n_sample)
    def _():
        _sample_step(t, *s_in, os_ref, *s_scr, n_p=n_p, n_steps=n_sample, nh=nh, c1=c1, layer=layer)

    @pl.when(t < n_prompt)
    def _():
        _prompt_iter(tbl_ref, t, *p_in, op_ref, *p_scr, tq=tq, hg=hg, c1=c1)


def _attn_call(pr, sm, ck, cv, layer, c_all, w_ada, b_ada, ada_layer, tq=512, hg=4, tp=512):
    bp, s, aw = pr["q"].shape
    bs, l, _ = sm["q"].shape
    nh = aw // HEAD_DIM
    ng, nq, gw = nh // hg, s // tq, hg * HEAD_DIM
    n_p = ck.shape[2] // tp
    sched = [(b, g, qi, kj) for b in range(bp) for g in range(ng) for qi in range(nq) for kj in range(qi + 1)]
    n_prompt, n_sample = len(sched), bs * n_p
    n_steps = max(n_prompt, n_sample)
    sched += [sched[-1]] * (n_steps - n_prompt)
    tbl = jnp.asarray(sched, jnp.int32).T

    qspec = pl.BlockSpec((1, tq, gw), lambda t, tb: (tb[0, t], tb[2, t], tb[1, t]))
    kvspec = pl.BlockSpec((1, s, gw), lambda t, tb: (tb[0, t], 0, tb[1, t]), pipeline_mode=pl.Buffered(1))
    frow4 = pr["frow"].reshape(bp * ng, hg, nq, tq).transpose(0, 2, 1, 3)
    prompt_specs = [qspec, kvspec, kvspec,
                    pl.BlockSpec((1, tq, nh), lambda t, tb: (tb[0, t], tb[2, t], 0)),
                    pl.BlockSpec((1, nq, hg, tq), lambda t, tb: (tb[0, t] * ng + tb[1, t], 0, 0, 0)),
                    qspec]

    def sb(t):
        return jnp.minimum(t // n_p, bs - 1)

    new = pl.BlockSpec((1, l, aw), lambda t, tb: (sb(t), 0, 0))
    cache = pl.BlockSpec(memory_space=pl.ANY)
    sample_specs = [new, new, new, cache, cache,
                    pl.BlockSpec((1, l, nh), lambda t, tb: (sb(t), 0, 0)),
                    pl.BlockSpec((1, nh, l), lambda t, tb: (sb(t), 0, 0)),
                    pl.BlockSpec((1, nh, tp), lambda t, tb: (sb(t), 0, jnp.where(t < n_sample, t % n_p, n_p - 1))),
                    new]

    prep = pltpu.VMEM((hg, tq, LANE), F32)
    prompt_scratch = [pltpu.VMEM((hg, tq, tq), F32), pltpu.VMEM((hg, tq, tq), BF16)] + [prep] * 6
    cbuf = pltpu.VMEM((2, nh, tp, HEAD_DIM), ck.dtype)
    csem = pltpu.SemaphoreType.DMA((2, nh))
    srep = pltpu.VMEM((nh, l, LANE), F32)
    sample_scratch = [cbuf, cbuf, csem, csem,
                      pltpu.VMEM((nh, l, tp), F32), pltpu.VMEM((nh, l, tp), BF16), srep,
                      pltpu.VMEM((nh, l, l), F32), pltpu.VMEM((nh, l, l), BF16),
                      pltpu.VMEM((nh, l, min(l, LANE)), F32)] + [srep] * 5
    assert len(prompt_specs) == N_PROMPT_IN and len(sample_specs) == N_SAMPLE_IN
    assert len(prompt_scratch) == N_PROMPT_SCRATCH

    rp, d = c_all.shape
    n_mod = w_ada.shape[2]
    n_ada = n_mod // LANE
    assert n_ada <= n_steps

    def at(t):
        return jnp.minimum(t, n_ada - 1)

    ada_specs = [pl.BlockSpec((rp, d), lambda t, tb: (0, 0)),
                 pl.BlockSpec((1, d, LANE), lambda t, tb: (ada_layer, 0, at(t))),
                 pl.BlockSpec((1, 1, LANE), lambda t, tb: (ada_layer, 0, at(t)))]
    ada_out = pl.BlockSpec((1, rp, LANE), lambda t, tb: (0, 0, at(t)))

    return pl.pallas_call(
        functools.partial(_attn_body, n_prompt=n_prompt, n_sample=n_sample, n_ada=n_ada, n_p=n_p, tq=tq, hg=hg,
                          nh=nh, c1=HEAD_DIM ** -0.5 * LOG2E, layer=layer),
        grid_spec=pltpu.PrefetchScalarGridSpec(
            num_scalar_prefetch=1, grid=(n_steps,),
            in_specs=prompt_specs + sample_specs + ada_specs,
            out_specs=[qspec, new, ada_out],
            scratch_shapes=prompt_scratch + sample_scratch),
        out_shape=[jax.ShapeDtypeStruct((bp, s, aw), BF16), jax.ShapeDtypeStruct((bs, l, aw), BF16),
                   jax.ShapeDtypeStruct((1, rp, n_mod), F32)],
        compiler_params=_params(("arbitrary",), 58),
        name="attn",
    )(tbl, pr["q"], pr["k"], pr["v"], pr["fcol"], frow4, pr["sgate"],
      sm["q"], sm["k"], sm["v"], ck, cv, sm["fcol"], sm["frow"], sm["gpast"], sm["sgate"],
      c_all, w_ada, b_ada.reshape(w_ada.shape[0], 1, n_mod))


def _pool_body(u_ref, halo_ref, hist_ref, sg_ref, w_ref, ls_ref, o_ref, ext_ref, *, pos0):
    r = pl.program_id(1)
    tm = u_ref.shape[1]
    group = u_ref.shape[2] // len(POOL_WINDOWS)
    ext_ref[HALO:HALO + tm, :] = u_ref[0]

    @pl.when(r == 0)
    def _():
        ext_ref[0:HALO, :] = hist_ref[0]

    @pl.when(r > 0)
    def _():
        ext_ref[0:HALO, :] = halo_ref[0]

    n_before = lax.broadcasted_iota(jnp.int32, (tm, 1), 0) + (pos0 + 1) + r * tm
    for gi, w in enumerate(POOL_WINDOWS):
        cs = slice(gi * group, (gi + 1) * group)
        win = ext_ref[HALO:HALO + tm, cs]
        for i in range(1, w):
            win = win + ext_ref[HALO - i:HALO - i + tm, cs]
        cnt = jnp.minimum(w, n_before).astype(F32)
        d = win / cnt - u_ref[0, :, cs]
        y = _dot(d.astype(BF16), w_ref[gi]) * ls_ref[:, cs]
        o_ref[0, :, cs] = (y * sg_ref[0, :, cs]).astype(BF16)


def _pool_call(u, hist16, sgate, w_pool, ls_pool, pos0, tm):
    b, l, bw = u.shape
    g = w_pool.shape[0]
    row = pl.BlockSpec((1, tm, bw), lambda bi, r: (bi, r, 0))
    halo_blocks = tm // HALO
    return pl.pallas_call(
        functools.partial(_pool_body, pos0=pos0),
        grid=(b, l // tm),
        in_specs=[row,
                  pl.BlockSpec((1, HALO, bw), lambda bi, r: (bi, jnp.maximum(r * halo_blocks - 1, 0), 0)),
                  pl.BlockSpec((1, HALO, bw), lambda bi, r: (bi, 0, 0)),
                  pl.BlockSpec((1, tm, bw), lambda bi, r: (bi, r, 1)),
                  pl.BlockSpec((g, bw // g, bw // g), lambda bi, r: (0, 0, 0)),
                  pl.BlockSpec((1, bw), lambda bi, r: (0, 0))],
        out_specs=row,
        out_shape=jax.ShapeDtypeStruct((b, l, bw), BF16),
        scratch_shapes=[pltpu.VMEM((HALO + tm, bw), F32)],
        compiler_params=_params(("arbitrary", "arbitrary"), 40),
        name="pool_mix",
    )(u, u, hist16, sgate, w_pool, ls_pool.reshape(1, bw))


def _sgu_body(u_ref, v_ref, gt_ref, gv_ref, bv_ref, ws_ref, bst_ref, o_ref, *vout, cl):
    lb, cw = u_ref.shape[1], u_ref.shape[2]
    gw = cw // N_SGU_GROUPS
    rr = lax.broadcasted_iota(jnp.int32, (cl, cl), 0)
    cc = lax.broadcasted_iota(jnp.int32, (cl, cl), 1)
    for c in range(lb // cl):
        rows = slice(c * cl, (c + 1) * cl)
        v = v_ref[0, rows, :].astype(F32)
        xc = v - jnp.mean(v, axis=-1, keepdims=True)
        var = jnp.mean(xc * xc, axis=-1, keepdims=True)
        vln = xc * lax.rsqrt(var + EPS) * gv_ref[...] + bv_ref[...]
        if vout:
            vout[0][0, rows, :] = vln
        for g in range(N_SGU_GROUPS):
            cs = slice(g * gw, (g + 1) * gw)
            ws = jnp.where(cc <= rr, ws_ref[g, :cl, :cl], 0.0).astype(BF16)
            sv = _dot(ws, vln[:, cs].astype(BF16)) + bst_ref[:cl, g:g + 1]
            o_ref[0, rows, cs] = (u_ref[0, rows, cs] * sv * gt_ref[0, rows, cs]).astype(BF16)


def _sgu_call(zact, g_v, b_v, w_s, b_s_t, lb, cl, want_v):
    b, l, cw3 = zact.shape
    cw = cw3 // 3
    out_spec = pl.BlockSpec((1, lb, cw), lambda bi, r: (bi, r, 0))
    out_shape = [jax.ShapeDtypeStruct((b, l, cw), BF16)]
    out_specs = [out_spec]
    if want_v:
        out_shape.append(jax.ShapeDtypeStruct((b, l, cw), F32))
        out_specs.append(out_spec)
    vec = pl.BlockSpec((1, cw), lambda bi, r: (0, 0))
    return pl.pallas_call(
        functools.partial(_sgu_body, cl=cl),
        grid=(b, l // lb),
        in_specs=[pl.BlockSpec((1, lb, cw), lambda bi, r: (bi, r, 0)),
                  pl.BlockSpec((1, lb, cw), lambda bi, r: (bi, r, 1)),
                  pl.BlockSpec((1, lb, cw), lambda bi, r: (bi, r, 2)),
                  vec, vec,
                  pl.BlockSpec(w_s.shape, lambda bi, r: (0, 0, 0)),
                  pl.BlockSpec(b_s_t.shape, lambda bi, r: (0, 0))],
        out_specs=out_specs,
        out_shape=out_shape,
        compiler_params=_params(("arbitrary", "arbitrary"), 48),
        name="sgu",
    )(zact, zact, zact, g_v.reshape(1, cw), b_v.reshape(1, cw), w_s, b_s_t)


def _row_blocking(b, l, rows=1024):
    if l >= rows:
        return 1, rows
    return rows // l, l


def _layer_ab_pre(x, shift, scale, g_norm, wts, clogf):
    b, l, d = x.shape
    col = wts["cols"]
    aw = col["q"][1]
    assert all(c[0] % 8 == 0 for c in col.values())
    nh = aw // HEAD_DIM
    h = _normmod_call(x, g_norm, scale, shift, *_row_blocking(b, l, ROWS_ELEMENTWISE)).reshape(b * l, d)

    proj = functools.partial(_proj_call, h, wts["win_t"], layer=wts["idx"], w_transposed=True)
    (qn,) = proj("headnorm", [BF16], extra=wts["gq"], cols=col["q"], name="proj_q")
    k32, k16 = proj("headnorm", [F32, BF16], extra=wts["gk"], cols=col["k"], name="proj_k")
    v32, v16 = proj("plain", [F32, BF16], cols=col["v"], name="proj_v")
    (logf,) = proj("logsigmoid", [F32], extra=wts["bf"], cols=col["f"], name="proj_f")
    (u,) = proj("plain", [F32], cols=col["u"], name="proj_u")
    (sgate,) = proj("silu", [BF16], cols=col["g"], name="proj_gate")

    logf = logf.reshape(b, l, nh)
    logf_t = logf.transpose(0, 2, 1).reshape(b * nh, l)
    frow = _cumsum_call(logf_t, False, min(b * nh, 64)).reshape(b, nh, l)
    grp = {"q": qn.reshape(b, l, aw), "k": k16.reshape(b, l, aw), "v": v16.reshape(b, l, aw),
           "frow": frow, "fcol": frow.transpose(0, 2, 1), "sgate": sgate.reshape(b, l, -1),
           "u": u.reshape(b, l, -1), "logf": logf,
           "k32": k32.reshape(b, l, nh, HEAD_DIM), "v32": v32.reshape(b, l, nh, HEAD_DIM)}
    if clogf is not None:
        p = clogf.shape[1]
        clf_t = clogf.transpose(0, 2, 1).reshape(b * nh, p)
        grp["gpast"] = _cumsum_call(clf_t, True, 64).reshape(b, nh, p)
    return grp


def _layer_ab_post(x, gate, grp, mixed_a, wts, hist, pos0):
    b, l, _ = x.shape
    bb, lb = _row_blocking(b, l)
    u3 = grp["u"]
    hist16 = jnp.pad(hist, ((0, 0), (HALO - POOL_HIST, 0), (0, 0)))
    mixed_b = _pool_call(u3, hist16, grp["sgate"], wts["wpool"], wts["lspool"], pos0, min(l, ROWS_ELEMENTWISE))
    y = _outproj_call([mixed_a, mixed_b], wts["wo"], wts["idx"], x, gate, bb, lb)
    if l >= POOL_HIST:
        new_hist = u3[:, l - POOL_HIST:]
    else:
        new_hist = jnp.concatenate([hist, u3], axis=1)[:, -POOL_HIST:]
    return y, grp["k32"], grp["v32"], grp["logf"], new_hist


def _layer_c(x, shift, scale, gate, g_norm, wts, want_v):
    b, l, d = x.shape
    bb, lb = _row_blocking(b, l)
    h = _normmod_call(x, g_norm, scale, shift, *_row_blocking(b, l, ROWS_ELEMENTWISE)).reshape(b * l, d)
    cw = wts["wo"].shape[1]
    tn = 512
    (zact,) = _proj_call(h, wts["win"], "gelu_silu", [BF16], n_gelu_tiles=2 * cw // tn, tn=tn,
                         layer=wts["idx"], name="proj_c")
    cl = min(l, SGU_CHUNK)
    outs = _sgu_call(zact.reshape(b, l, 3 * cw), wts["gv"], wts["bv"], wts["ws"], wts["bst"],
                     min(l, ROWS_ELEMENTWISE), cl, want_v)
    y = _outproj_call([outs[0]], wts["wo"], wts["idx"], x, gate, bb, lb)
    return y, (outs[1] if want_v else None)


def kernel(x_prompt, x_sample, cache_k, cache_v, cache_logf, state_pool, c_prompt, c_sample,
           w_ada, b_ada, g_norm, w_in_ab, b_forget, g_q, g_k, w_pool, ls_pool, w_out_ab,
           w_in_c, g_v, b_v, w_s, b_s, w_out_c):
    bp, sp, d = x_prompt.shape
    bs = x_sample.shape[0]
    depth = w_ada.shape[0]
    nh = cache_k.shape[3]
    aw = nh * HEAD_DIM
    bw = w_pool.shape[2] * w_pool.shape[1]
    past_len = cache_k.shape[2]

    c_all = jnp.concatenate([c_prompt, c_sample], axis=0)
    c_all = jnp.pad(c_all, ((0, -c_all.shape[0] % 8), (0, 0)))
    mod_even = _ada_call(c_all, w_ada, b_ada, 2)
    mod_by_layer = {2 * i: mod_even[i] for i in range(mod_even.shape[0])}

    def mods(layer, lo, n):
        m = mod_by_layer[layer][lo:lo + n].reshape(n, 1, 3 * d)
        return m[..., :d], m[..., d:2 * d], m[..., 2 * d:]

    yp, ys = x_prompt, x_sample
    outs_p, outs_s, sgu_v = [], [], []
    for layer in range(depth):
        i = layer // 2
        shp, scp, gp = mods(layer, 0, bp)
        shs, scs, gs = mods(layer, bp, bs)
        if layer % 2 == 0:
            o_f, o_u, o_g = 3 * aw, 3 * aw + nh, 3 * aw + nh + bw
            wts = {
                "win_t": jnp.swapaxes(w_in_ab, 1, 2),
                "cols": {"q": (0, aw), "k": (aw, aw), "v": (2 * aw, aw), "f": (o_f, nh), "u": (o_u, bw),
                         "g": (o_g, w_in_ab.shape[2] - o_g)},
                "bf": b_forget[i].reshape(1, nh),
                "gq": g_q[i].reshape(1, HEAD_DIM),
                "gk": g_k[i].reshape(1, HEAD_DIM),
                "wpool": w_pool[i].astype(BF16),
                "lspool": ls_pool[i],
                "wo": w_out_ab, "idx": i,
            }
            zero_hist = jnp.zeros((bp, POOL_HIST, bw), F32)
            grp_p = _layer_ab_pre(yp, shp, scp, g_norm[layer], wts, None)
            grp_s = _layer_ab_pre(ys, shs, scs, g_norm[layer], wts, cache_logf[i])
            nxt = min(layer + 1, depth - 1)
            mixed_p, mixed_s, mod_nxt = _attn_call(grp_p, grp_s, cache_k, cache_v, i, c_all, w_ada, b_ada, nxt)
            mod_by_layer[nxt] = mod_nxt[0]
            rp = _layer_ab_post(yp, gp, grp_p, mixed_p, wts, zero_hist, 0)
            rs = _layer_ab_post(ys, gs, grp_s, mixed_s, wts, state_pool[i], past_len)
            yp, ys = rp[0], rs[0]
            outs_p.append(rp[1:])
            outs_s.append(rs[1:])
        else:
            wts = {
                "win": w_in_c, "wo": w_out_c, "idx": i,
                "gv": g_v[i], "bv": b_v[i],
                "ws": w_s[i], "bst": b_s[i].T,
            }
            yp, _ = _layer_c(yp, shp, scp, gp, g_norm[layer], wts, False)
            ys, v_c = _layer_c(ys, shs, scs, gs, g_norm[layer], wts, True)
            sgu_v.append(v_c)

    def stack(group, idx):
        return jnp.stack([o[idx] for o in group])

    return (yp, ys,
            stack(outs_p, 0), stack(outs_p, 1), stack(outs_p, 2), stack(outs_p, 3),
            stack(outs_s, 0), stack(outs_s, 1), stack(outs_s, 2), stack(outs_s, 3),
            jnp.stack(sgu_v))
```
